```python
import math
import jax
import jax.numpy as jnp
from jax import lax
import numpy as np


D_MODEL = 1024
BATCH = 8
SEQ = 4096
DEPTH = 2

CHUNK = 64
Q_BLOCK = 128
EPS = 1e-6
N_BRANCHES = 4
BRANCH_WIDTH = D_MODEL // 4
HEADS = 4
HEAD_DIM = BRANCH_WIDTH // HEADS
D_FF = 4 * D_MODEL

DN_DK = HEAD_DIM
DN_DV = HEAD_DIM
CONV_K = 4
MLA_NOPE = HEAD_DIM
MLA_ROPE = HEAD_DIM // 2
MLA_V = HEAD_DIM
MLA_Q_LORA = D_MODEL // 4
MLA_KV_LORA = D_MODEL // 8
ROPE_THETA = 10000.0
GLA_DK = HEAD_DIM // 2
GLA_DV = HEAD_DIM
GLA_RANK = 16
GLA_TAU = 16.0
FOX_DH = HEAD_DIM

IN_SIZES = (
    HEADS * (2 * DN_DK + DN_DV), HEADS * DN_DV, HEADS, HEADS,
    MLA_Q_LORA, MLA_KV_LORA, MLA_ROPE,
    HEADS * GLA_DK, HEADS * GLA_DK, HEADS * GLA_DV, HEADS * GLA_DV, GLA_RANK,
    HEADS * FOX_DH, HEADS * FOX_DH, HEADS * FOX_DH, HEADS,
)
IN_WIDTH = sum(IN_SIZES)

kernel_name = 'hybrid_gated_parallel_mixer_block'


def rms_norm(x, gain):
    xf = x.astype(jnp.float32)
    y = xf * lax.rsqrt(jnp.mean(xf * xf, axis=-1, keepdims=True) + EPS)
    return (y * gain.astype(jnp.float32)).astype(x.dtype)


def l2_normalize(x):
    xf = x.astype(jnp.float32)
    return (xf * lax.rsqrt(jnp.sum(xf * xf, axis=-1, keepdims=True) + EPS)).astype(x.dtype)


def causal_depthwise_conv(x, w):
    k = w.shape[0]
    return lax.conv_general_dilated(
        x, w[:, None, :].astype(x.dtype), window_strides=(1,), padding=((k - 1, 0),),
        dimension_numbers=('NWC', 'WIO', 'NWC'), feature_group_count=x.shape[-1])


def to_chunks(t):
    b, s, h, d = t.shape
    return t.reshape(b, s // CHUNK, CHUNK, h, d).transpose(0, 3, 1, 2, 4)


def from_chunks(t):
    b, h, n, c, d = t.shape
    return t.transpose(0, 2, 3, 1, 4).reshape(b, n * c, h, d)


def scalar_chunks(t):
    b, s, h = t.shape
    return t.reshape(b, s // CHUNK, CHUNK, h).transpose(0, 3, 1, 2)


def gated_delta_rule(q, k, v, g, beta):
    out_dtype = v.dtype
    f32 = jnp.float32
    b_sz, s_len, n_h, dk = q.shape
    dv = v.shape[-1]
    qc = to_chunks(q.astype(f32)) * (dk ** -0.5)
    kc = to_chunks(k.astype(f32))
    vc = to_chunks(v.astype(f32))
    gc = jnp.cumsum(scalar_chunks(g.astype(f32)), axis=-1)
    bc = scalar_chunks(beta.astype(f32))
    lower = jnp.tril(jnp.ones((CHUNK, CHUNK), dtype=bool))
    strict = jnp.tril(jnp.ones((CHUNK, CHUNK), dtype=bool), -1)
    diff = gc[..., :, None] - gc[..., None, :]
    decay_ts = jnp.where(lower, jnp.exp(jnp.where(lower, diff, 0.0)), 0.0)
    kk = jnp.einsum('bhncd,bhnsd->bhncs', kc, kc)
    a_strict = jnp.where(strict, bc[..., None] * kk * decay_ts, 0.0)
    eye = jnp.eye(CHUNK, dtype=f32)
    rhs = jnp.concatenate([vc * bc[..., None], kc * (bc * jnp.exp(gc))[..., None]], axis=-1)
    sol = lax.linalg.triangular_solve(a_strict + eye, rhs, left_side=True, lower=True,
                                      unit_diagonal=True)
    u = sol[..., :dv]
    w = sol[..., dv:]
    qk = jnp.einsum('bhncd,bhnsd->bhncs', qc, kc) * decay_ts
    q_decay = qc * jnp.exp(gc)[..., None]
    k_tail = kc * jnp.exp(gc[..., -1:] - gc)[..., None]
    chunk_decay = jnp.exp(gc[..., -1])

    def step(state, inp):
        u_c, w_c, qk_c, qd_c, kt_c, cd_c = inp
        v_new = u_c - jnp.einsum('bhck,bhkv->bhcv', w_c, state)
        o_c = (jnp.einsum('bhck,bhkv->bhcv', qd_c, state)
               + jnp.einsum('bhcs,bhsv->bhcv', qk_c, v_new))
        state = state * cd_c[..., None, None] + jnp.einsum('bhck,bhcv->bhkv', kt_c, v_new)
        return state, o_c

    xs = tuple(jnp.moveaxis(t, 2, 0) for t in (u, w, qk, q_decay, k_tail, chunk_decay))
    s0 = jnp.zeros((b_sz, n_h, dk, dv), f32)
    _, o = lax.scan(step, s0, xs)
    return from_chunks(jnp.moveaxis(o, 0, 2)).astype(out_dtype)


def gla_chunked(q, k, v, log_a):
    out_dtype = v.dtype
    f32 = jnp.float32
    b_sz, s_len, n_h, dk = q.shape
    dv = v.shape[-1]
    qc = to_chunks(q.astype(f32)) * (dk ** -0.5)
    kc = to_chunks(k.astype(f32))
    vc = to_chunks(v.astype(f32))
    bcum = jnp.cumsum(to_chunks(log_a.astype(f32)), axis=3)
    q_in = qc * jnp.exp(bcum)
    k_in = kc * jnp.exp(-bcum)
    lower = jnp.tril(jnp.ones((CHUNK, CHUNK), dtype=bool))
    attn = jnp.where(lower, jnp.einsum('bhncd,bhnsd->bhncs', q_in, k_in), 0.0)
    o_intra = jnp.einsum('bhncs,bhnsv->bhncv', attn, vc)
    b_last = bcum[..., -1:, :]
    k_tail = kc * jnp.exp(b_last - bcum)
    chunk_decay = jnp.exp(b_last[..., 0, :])

    def step(state, inp):
        qi_c, kt_c, v_c, cd_c = inp
        o_c = jnp.einsum('bhck,bhkv->bhcv', qi_c, state)
        state = state * cd_c[..., None] + jnp.einsum('bhck,bhcv->bhkv', kt_c, v_c)
        return state, o_c

    xs = tuple(jnp.moveaxis(t, 2, 0) for t in (q_in, k_tail, vc, chunk_decay))
    s0 = jnp.zeros((b_sz, n_h, dk, dv), f32)
    _, o_inter = lax.scan(step, s0, xs)
    o = o_intra + jnp.moveaxis(o_inter, 0, 2)
    return from_chunks(o).astype(out_dtype)


def rope_cos_sin(seq, dim):
    inv_freq = 1.0 / (ROPE_THETA ** (jnp.arange(0, dim, 2, dtype=jnp.float32) / dim))
    ang = jnp.arange(seq, dtype=jnp.float32)[:, None] * inv_freq[None, :]
    ang = jnp.concatenate([ang, ang], axis=-1)
    return jnp.cos(ang), jnp.sin(ang)


def apply_rope(x, cos, sin):
    xf = x.astype(jnp.float32)
    x1, x2 = jnp.split(xf, 2, axis=-1)
    rot = jnp.concatenate([-x2, x1], axis=-1)
    return (xf * cos[None, :, None, :] + rot * sin[None, :, None, :]).astype(x.dtype)


def block_sweep_attention(q, k, v, scale, forget_cum):
    s_len = q.shape[1]
    outs = []
    for start in range(0, s_len, Q_BLOCK):
        end = start + Q_BLOCK
        logits = jnp.einsum('bqhd,bkhd->bhqk', q[:, start:end], k[:, :end],
                            preferred_element_type=jnp.float32) * scale
        q_pos = jnp.arange(start, end)
        k_pos = jnp.arange(end)
        if forget_cum is None:
            allowed = (k_pos[None, :] // CHUNK) <= (q_pos[:, None] // CHUNK)
        else:
            allowed = k_pos[None, :] <= q_pos[:, None]
            f_q = forget_cum[:, start:end].transpose(0, 2, 1)
            f_k = forget_cum[:, :end].transpose(0, 2, 1)
            logits = logits + (f_q[..., :, None] - f_k[..., None, :])
        logits = jnp.where(allowed, logits, -jnp.inf)
        p = jax.nn.softmax(logits, axis=-1)
        outs.append(jnp.einsum('bhqk,bkhd->bqhd', p.astype(v.dtype), v[:, :end]))
    return jnp.concatenate(outs, axis=1)


def hybrid_layer(x, norm_mix_pre, norm_mix_post, norm_mlp_pre, norm_mlp_post, w_in,
                 dn_conv, dn_a_log, dn_dt_bias, dn_norm,
                 mla_q_norm, mla_w_q_up, mla_kv_norm, mla_w_kv_up,
                 gla_w_gate_up, gla_gate_bias, gla_norm, fox_f_bias,
                 w_branch_up, w_gate, b_gate, w_out, w_mlp_in, w_mlp_out):
    f32 = jnp.float32
    b_sz, s_len, _ = x.shape
    h = rms_norm(x, norm_mix_pre)
    proj = h @ w_in
    split_at = np.cumsum(IN_SIZES)[:-1].tolist()
    (dn_qkv, dn_z, dn_b, dn_a, mla_cq, mla_ckv, mla_kpe,
     gla_q, gla_k, gla_v, gla_g, gla_lr,
     fox_q, fox_k, fox_v, fox_f) = jnp.split(proj, split_at, axis=-1)

    qkv = jax.nn.silu(causal_depthwise_conv(dn_qkv, dn_conv))
    a_q, a_k, a_v = jnp.split(qkv, [HEADS * DN_DK, 2 * HEADS * DN_DK], axis=-1)
    a_q = l2_normalize(a_q.reshape(b_sz, s_len, HEADS, DN_DK))
    a_k = l2_normalize(a_k.reshape(b_sz, s_len, HEADS, DN_DK))
    a_v = a_v.reshape(b_sz, s_len, HEADS, DN_DV)
    beta = jax.nn.sigmoid(dn_b.astype(f32))
    g = -jnp.exp(dn_a_log.astype(f32)) * jax.nn.softplus(dn_a.astype(f32) + dn_dt_bias.astype(f32))
    o_a = gated_delta_rule(a_q, a_k, a_v, g, beta)
    o_a = (rms_norm(o_a, dn_norm) * jax.nn.silu(dn_z.reshape(b_sz, s_len, HEADS, DN_DV))
           ).reshape(b_sz, s_len, BRANCH_WIDTH)

    cq = rms_norm(mla_cq, mla_q_norm)
    b_q = (cq @ mla_w_q_up).reshape(b_sz, s_len, HEADS, MLA_NOPE + MLA_ROPE)
    q_nope, q_pe = jnp.split(b_q, [MLA_NOPE], axis=-1)
    ckv = rms_norm(mla_ckv, mla_kv_norm)
    b_kv = (ckv @ mla_w_kv_up).reshape(b_sz, s_len, HEADS, MLA_NOPE + MLA_V)
    k_nope, b_v = jnp.split(b_kv, [MLA_NOPE], axis=-1)
    cos, sin = rope_cos_sin(s_len, MLA_ROPE)
    q_pe = apply_rope(q_pe, cos, sin)
    k_pe = apply_rope(mla_kpe[:, :, None, :], cos, sin)
    b_qf = jnp.concatenate([q_nope, q_pe], axis=-1)
    b_kf = jnp.concatenate([k_nope, jnp.broadcast_to(k_pe, (b_sz, s_len, HEADS, MLA_ROPE))], axis=-1)
    o_b = block_sweep_attention(b_qf, b_kf, b_v, (MLA_NOPE + MLA_ROPE) ** -0.5, None)
    o_b = o_b.reshape(b_sz, s_len, BRANCH_WIDTH)

    log_a = jax.nn.log_sigmoid((gla_lr @ gla_w_gate_up + gla_gate_bias).astype(f32)) / GLA_TAU
    o_c = gla_chunked(gla_q.reshape(b_sz, s_len, HEADS, GLA_DK),
                      gla_k.reshape(b_sz, s_len, HEADS, GLA_DK),
                      gla_v.reshape(b_sz, s_len, HEADS, GLA_DV),
                      log_a.reshape(b_sz, s_len, HEADS, GLA_DK))
    o_c = (rms_norm(o_c, gla_norm) * jax.nn.silu(gla_g.reshape(b_sz, s_len, HEADS, GLA_DV))
           ).reshape(b_sz, s_len, BRANCH_WIDTH)

    log_f = jax.nn.log_sigmoid((fox_f + fox_f_bias).astype(f32))
    forget_cum = jnp.cumsum(log_f, axis=1)
    o_d = block_sweep_attention(fox_q.reshape(b_sz, s_len, HEADS, FOX_DH),
                                fox_k.reshape(b_sz, s_len, HEADS, FOX_DH),
                                fox_v.reshape(b_sz, s_len, HEADS, FOX_DH),
                                FOX_DH ** -0.5, forget_cum)
    o_d = o_d.reshape(b_sz, s_len, BRANCH_WIDTH)

    branches = (o_a, o_b, o_c, o_d)
    merged = jnp.zeros_like(x)
    for i in range(N_BRANCHES):
        gate = jax.nn.sigmoid(h @ w_gate[i] + b_gate[i])
        merged = merged + gate * (branches[i] @ w_branch_up[i])
    x = x + rms_norm(merged @ w_out, norm_mix_post)

    h2 = rms_norm(x, norm_mlp_pre)
    u = jax.nn.relu(h2 @ w_mlp_in)
    x = x + rms_norm((u * u) @ w_mlp_out, norm_mlp_post)
    return x


def setup_inputs(seed: int = 0) -> dict:
    key = jax.random.key(seed)
    ks = jax.random.split(key, 24)
    L, D, f32 = DEPTH, D_MODEL, jnp.float32

    def nrm(i, shape, scale):
        return jax.random.normal(ks[i], shape, f32) * scale

    def gain(i, shape):
        return 1.0 + 0.1 * jax.random.normal(ks[i], shape, f32)

    conv_width = HEADS * (2 * DN_DK + DN_DV)
    dt = jnp.exp(jax.random.uniform(ks[7], (L, HEADS), f32, math.log(1e-3), math.log(1e-1)))
    return {
        'x': nrm(0, (BATCH, SEQ, D), 1.0),
        'norm_mix_pre': gain(1, (L, D)),
        'norm_mix_post': gain(2, (L, D)),
        'norm_mlp_pre': gain(3, (L, D)),
        'norm_mlp_post': gain(4, (L, D)),
        'w_in': nrm(5, (L, D, IN_WIDTH), D ** -0.5),
        'dn_conv': nrm(8, (L, CONV_K, conv_width), CONV_K ** -0.5),
        'dn_a_log': jnp.log(jax.random.uniform(ks[6], (L, HEADS), f32, 1.0, 16.0)),
        'dn_dt_bias': dt + jnp.log(-jnp.expm1(-dt)),
        'dn_norm': gain(9, (L, DN_DV)),
        'mla_q_norm': gain(10, (L, MLA_Q_LORA)),
        'mla_w_q_up': nrm(11, (L, MLA_Q_LORA, HEADS * (MLA_NOPE + MLA_ROPE)), MLA_Q_LORA ** -0.5),
        'mla_kv_norm': gain(12, (L, MLA_KV_LORA)),
        'mla_w_kv_up': nrm(13, (L, MLA_KV_LORA, HEADS * (MLA_NOPE + MLA_V)), MLA_KV_LORA ** -0.5),
        'gla_w_gate_up': nrm(14, (L, GLA_RANK, HEADS * GLA_DK), GLA_RANK ** -0.5),
        'gla_gate_bias': nrm(15, (L, HEADS * GLA_DK), 0.1),
        'gla_norm': gain(16, (L, GLA_DV)),
        'fox_f_bias': jax.random.uniform(ks[17], (L, HEADS), f32, 1.0, 3.0),
        'w_branch_up': nrm(18, (L, N_BRANCHES, BRANCH_WIDTH, D), BRANCH_WIDTH ** -0.5),
        'w_gate': nrm(19, (L, N_BRANCHES, D, D), D ** -0.5),
        'b_gate': nrm(20, (L, N_BRANCHES, D), 0.1),
        'w_out': nrm(21, (L, D, D), D ** -0.5),
        'w_mlp_in': nrm(22, (L, D, D_FF), D ** -0.5),
        'w_mlp_out': nrm(23, (L, D_FF, D), D_FF ** -0.5),
    }


def reference(x, norm_mix_pre, norm_mix_post, norm_mlp_pre, norm_mlp_post, w_in,
              dn_conv, dn_a_log, dn_dt_bias, dn_norm,
              mla_q_norm, mla_w_q_up, mla_kv_norm, mla_w_kv_up,
              gla_w_gate_up, gla_gate_bias, gla_norm, fox_f_bias,
              w_branch_up, w_gate, b_gate, w_out, w_mlp_in, w_mlp_out):
    for l in range(DEPTH):
        x = hybrid_layer(x, norm_mix_pre[l], norm_mix_post[l], norm_mlp_pre[l], norm_mlp_post[l],
                         w_in[l], dn_conv[l], dn_a_log[l], dn_dt_bias[l], dn_norm[l],
                         mla_q_norm[l], mla_w_q_up[l], mla_kv_norm[l], mla_w_kv_up[l],
                         gla_w_gate_up[l], gla_gate_bias[l], gla_norm[l], fox_f_bias[l],
                         w_branch_up[l], w_gate[l], b_gate[l], w_out[l], w_mlp_in[l], w_mlp_out[l])
    return x
```

```python
import functools
import math

import jax
import jax.numpy as jnp
import numpy as np
from jax import lax
from jax.experimental import pallas as pl
from jax.experimental.pallas import tpu as pltpu

F32 = jnp.float32
BF16 = jnp.bfloat16

D_MODEL = 1024
CHUNK = 64
EPS = 1e-6
HEADS = 4
HEAD_DIM = 64
BRANCH_WIDTH = HEADS * HEAD_DIM
D_FF = 4 * D_MODEL
CONV_K = 4
MLA_NOPE = HEAD_DIM
MLA_ROPE = HEAD_DIM // 2
MLA_Q_LORA = D_MODEL // 4
MLA_KV_LORA = D_MODEL // 8
ROPE_THETA = 10000.0
GLA_DK = HEAD_DIM // 2
GLA_RANK = 16
GLA_TAU = 16.0
LANE = 128
SLOT = LANE

A_WIDTH = 3 * BRANCH_WIDTH + 3 * BRANCH_WIDTH
B_WIDTH = MLA_Q_LORA + MLA_KV_LORA + 2 * SLOT
C_WIDTH = 2 * HEADS * GLA_DK + 2 * BRANCH_WIDTH + LANE
D_WIDTH = 3 * BRANCH_WIDTH + LANE
IN_TOTAL = A_WIDTH + B_WIDTH + C_WIDTH + D_WIDTH

VMEM_LIMIT = 56 * 1024 * 1024


def _dot(a, b):
    return jnp.dot(a, b, preferred_element_type=F32)


def _dot_nt(a, b):
    return lax.dot_general(a, b, (((1,), (1,)), ((), ())), preferred_element_type=F32)


def _dot_tn(a, b):
    return lax.dot_general(a, b, (((0,), (0,)), ((), ())), preferred_element_type=F32)


def _dot_hi(a, b):
    return jnp.dot(a, b, preferred_element_type=F32, precision=lax.Precision.HIGHEST)


def _bf(x):
    return x.astype(BF16)


def _rms(x, gain):
    return x * lax.rsqrt(jnp.mean(x * x, axis=-1, keepdims=True) + EPS) * gain


def _silu(x):
    return x * jax.nn.sigmoid(x)


def _softplus(x):
    return jnp.maximum(x, 0.0) + jnp.log1p(jnp.exp(-jnp.abs(x)))


def _log_sigmoid(x):
    return jnp.minimum(x, 0.0) - jnp.log1p(jnp.exp(-jnp.abs(x)))


def _tri(n, strict=False):
    r = lax.broadcasted_iota(jnp.int32, (n, n), 0)
    c = lax.broadcasted_iota(jnp.int32, (n, n), 1)
    return (c < r) if strict else (c <= r)


def _params(*sem):
    return pltpu.CompilerParams(dimension_semantics=sem, vmem_limit_bytes=VMEM_LIMIT)


def _const_spec(shape):
    nd = len(shape)
    return pl.BlockSpec(shape, lambda *_: (0,) * nd, pipeline_mode=pl.Buffered(1))


def _inproj_kernel(x_ref, gain_ref, w_ref, convw_ref, wgu_ref, gbias_ref,
                   oa_ref, ob_ref, oc_ref, od_ref, xe_ref, *, tm, tiles_per_seq):
    i = pl.program_id(0)
    h = _bf(_rms(x_ref[...], gain_ref[...]))

    qkv_w = 3 * BRANCH_WIDTH
    pa = _dot(h, w_ref[:, 0:A_WIDTH])
    raw = pa[:, 0:qkv_w]

    @pl.when(i % tiles_per_seq == 0)
    def _():
        xe_ref[0:8, :] = jnp.zeros((8, qkv_w), F32)

    xe_ref[8:8 + tm, :] = raw
    y = jnp.zeros((tm, qkv_w), F32)
    for j in range(CONV_K):
        y = y + convw_ref[j:j + 1, :] * xe_ref[pl.ds(8 - (CONV_K - 1) + j, tm), :]
    tail = xe_ref[tm:tm + 8, :]
    xe_ref[0:8, :] = tail
    oa_ref[:, 0:qkv_w] = _silu(y)
    oa_ref[:, qkv_w:A_WIDTH] = pa[:, qkv_w:A_WIDTH]

    off = A_WIDTH
    ob_ref[...] = _dot(h, w_ref[:, off:off + B_WIDTH])
    off += B_WIDTH
    pc = _dot(h, w_ref[:, off:off + C_WIDTH])
    lr_off = C_WIDTH - LANE
    oc_ref[:, 0:lr_off] = pc[:, 0:lr_off]
    gate = _dot(_bf(pc[:, lr_off:C_WIDTH]), wgu_ref[...]) + gbias_ref[...]
    oc_ref[:, lr_off:C_WIDTH] = _log_sigmoid(gate) * (1.0 / GLA_TAU)
    off += C_WIDTH
    od_ref[...] = _dot(h, w_ref[:, off:off + D_WIDTH])


def _inproj(x2, gain, w_cat, convw, wgu, gbias, seq, tm=256):
    t = x2.shape[0]
    grid = (t // tm,)
    row = lambda w: pl.BlockSpec((tm, w), lambda i: (i, 0))
    return pl.pallas_call(
        functools.partial(_inproj_kernel, tm=tm, tiles_per_seq=seq // tm),
        grid=grid,
        in_specs=[row(D_MODEL), _const_spec((1, D_MODEL)), _const_spec((D_MODEL, IN_TOTAL)),
                  _const_spec((CONV_K, 3 * BRANCH_WIDTH)), _const_spec((LANE, LANE)),
                  _const_spec((1, LANE))],
        out_specs=[row(A_WIDTH), row(B_WIDTH), row(C_WIDTH), row(D_WIDTH)],
        out_shape=[jax.ShapeDtypeStruct((t, w), F32) for w in (A_WIDTH, B_WIDTH, C_WIDTH, D_WIDTH)],
        scratch_shapes=[pltpu.VMEM((tm + 8, 3 * BRANCH_WIDTH), F32)],
        compiler_params=_params("arbitrary"),
        name="inproj",
    )(x2, gain, w_cat, convw, wgu, gbias)


def _inv_unit_lower(a):
    n = a.shape[0]
    eye = jnp.where(_tri(n) & ~_tri(n, True), 1.0, 0.0).astype(F32)
    t = eye - a
    p = a
    for _ in range(int(math.log2(n)) - 1):
        p = _dot_hi(p, p)
        t = t + _dot_hi(t, p)
    return t


def _deltanet_kernel(a_ref, alog_ref, dtb_ref, norm_ref, o_ref, state_ref, *, tb, blocks_per_seq):
    i = pl.program_id(0)

    @pl.when(i % blocks_per_seq == 0)
    def _():
        state_ref[...] = jnp.zeros_like(state_ref)

    bw = BRANCH_WIDTH
    lower = _tri(CHUNK)
    strict = _tri(CHUNK, True)
    lmat = jnp.where(lower, 1.0, 0.0).astype(F32)
    smat = jnp.where(strict, 1.0, 0.0).astype(F32)
    neg_a = -jnp.exp(alog_ref[...])
    g_all = neg_a * _softplus(a_ref[:, 4 * bw:5 * bw] + dtb_ref[...])
    beta_all = jax.nn.sigmoid(a_ref[:, 5 * bw:6 * bw])

    for c in range(tb // CHUNK):
        rows = slice(c * CHUNK, (c + 1) * CHUNK)
        outs = []
        for h in range(HEADS):
            cols = slice(h * HEAD_DIM, (h + 1) * HEAD_DIM)
            q = a_ref[rows, h * HEAD_DIM:(h + 1) * HEAD_DIM]
            k = a_ref[rows, bw + h * HEAD_DIM:bw + (h + 1) * HEAD_DIM]
            v = a_ref[rows, 2 * bw + h * HEAD_DIM:2 * bw + (h + 1) * HEAD_DIM]
            z = a_ref[rows, 3 * bw + h * HEAD_DIM:3 * bw + (h + 1) * HEAD_DIM]
            g = g_all[rows, cols]
            beta = beta_all[rows, cols]
            q = q * lax.rsqrt(jnp.sum(q * q, axis=-1, keepdims=True) + EPS) * (HEAD_DIM ** -0.5)
            k = k * lax.rsqrt(jnp.sum(k * k, axis=-1, keepdims=True) + EPS)

            gc = _dot_hi(lmat, g)
            diff = _dot_hi(lmat, g * smat)
            decay = jnp.where(lower, jnp.exp(jnp.where(lower, diff, 0.0)), 0.0)
            kb = _bf(k)
            kk = _dot_nt(kb, kb)
            a_strict = jnp.where(strict, beta * kk * decay, 0.0)
            tinv = _inv_unit_lower(a_strict)
            egc = jnp.exp(gc)
            u = _dot_hi(tinv, v * beta)
            w = _dot_hi(tinv, k * beta * egc)
            qk = _dot_nt(_bf(q), kb) * decay
            q_decay = q * egc
            gc_last = gc[CHUNK - 1:CHUNK, :]
            k_tail = k * jnp.exp(gc_last - gc)
            cd = jnp.exp(gc_last)

            s = state_ref[h]
            sb = _bf(s)
            v_new = u - _dot(_bf(w), sb)
            vb = _bf(v_new)
            o = _dot(_bf(q_decay), sb) + _dot(_bf(qk), vb)
            state_ref[h] = s * cd + _dot_tn(_bf(k_tail), vb)

            o = _rms(o, norm_ref[...]) * _silu(z)
            outs.append(o)
        o_ref[rows, :] = _bf(jnp.concatenate(outs, axis=1))


def _deltanet(ga, alog, dtb, norm, seq, tb=256):
    t = ga.shape[0]
    return pl.pallas_call(
        functools.partial(_deltanet_kernel, tb=tb, blocks_per_seq=seq // tb),
        grid=(t // tb,),
        in_specs=[pl.BlockSpec((tb, A_WIDTH), lambda i: (i, 0)),
                  _const_spec((1, BRANCH_WIDTH)), _const_spec((1, BRANCH_WIDTH)),
                  _const_spec((1, HEAD_DIM))],
        out_specs=pl.BlockSpec((tb, BRANCH_WIDTH), lambda i: (i, 0)),
        out_shape=jax.ShapeDtypeStruct((t, BRANCH_WIDTH), BF16),
        scratch_shapes=[pltpu.VMEM((HEADS, HEAD_DIM, HEAD_DIM), F32)],
        compiler_params=_params("arbitrary"),
        name="deltanet",
    )(ga, alog, dtb, norm)


def _gla_kernel(c_ref, norm_ref, o_ref, state_ref, *, tb, blocks_per_seq):
    i = pl.program_id(0)

    @pl.when(i % blocks_per_seq == 0)
    def _():
        state_ref[...] = jnp.zeros_like(state_ref)

    qw = HEADS * GLA_DK
    bw = BRANCH_WIDTH
    lower = _tri(CHUNK)
    lmat = jnp.where(lower, 1.0, 0.0).astype(F32)
    la_off = 2 * qw + 2 * bw

    for c in range(tb // CHUNK):
        rows = slice(c * CHUNK, (c + 1) * CHUNK)
        outs = []
        for h in range(HEADS):
            q = c_ref[rows, h * GLA_DK:(h + 1) * GLA_DK] * (GLA_DK ** -0.5)
            k = c_ref[rows, qw + h * GLA_DK:qw + (h + 1) * GLA_DK]
            v = c_ref[rows, 2 * qw + h * HEAD_DIM:2 * qw + (h + 1) * HEAD_DIM]
            g = c_ref[rows, 2 * qw + bw + h * HEAD_DIM:2 * qw + bw + (h + 1) * HEAD_DIM]
            log_a = c_ref[rows, la_off + h * GLA_DK:la_off + (h + 1) * GLA_DK]

            bcum = _dot_hi(lmat, log_a)
            q_in = _bf(q * jnp.exp(bcum))
            k_in = _bf(k * jnp.exp(-bcum))
            attn = jnp.where(lower, _dot_nt(q_in, k_in), 0.0)
            vb = _bf(v)
            b_last = bcum[CHUNK - 1:CHUNK, :]
            k_tail = _bf(k * jnp.exp(b_last - bcum))

            st = state_ref[h]
            o = _dot(_bf(attn), vb) + _dot_nt(q_in, _bf(st))
            state_ref[h] = st * jnp.exp(b_last) + _dot_tn(vb, k_tail)

            outs.append(_rms(o, norm_ref[...]) * _silu(g))
        o_ref[rows, :] = _bf(jnp.concatenate(outs, axis=1))


def _gla(gc, norm, seq, tb=256):
    t = gc.shape[0]
    return pl.pallas_call(
        functools.partial(_gla_kernel, tb=tb, blocks_per_seq=seq // tb),
        grid=(t // tb,),
        in_specs=[pl.BlockSpec((tb, C_WIDTH), lambda i: (i, 0)), _const_spec((1, HEAD_DIM))],
        out_specs=pl.BlockSpec((tb, BRANCH_WIDTH), lambda i: (i, 0)),
        out_shape=jax.ShapeDtypeStruct((t, BRANCH_WIDTH), BF16),
        scratch_shapes=[pltpu.VMEM((HEADS, HEAD_DIM, GLA_DK), F32)],
        compiler_params=_params("arbitrary"),
        name="gla",
    )(gc, norm)


def _mla_prep_kernel(b_ref, qn_ref, kvn_ref, wq_ref, wkv_ref, cos_ref, sin_ref,
                     q_ref, k_ref, v_ref):
    cq = b_ref[:, 0:MLA_Q_LORA]
    ckv = b_ref[:, MLA_Q_LORA:MLA_Q_LORA + MLA_KV_LORA]
    off = MLA_Q_LORA + MLA_KV_LORA
    kpe = b_ref[:, off:off + SLOT]
    kpe_rot = b_ref[:, off + SLOT:off + 2 * SLOT]
    cos = cos_ref[...]
    sin = sin_ref[...]
    lane = lax.broadcasted_iota(jnp.int32, cos.shape, 1)
    cos_pe = jnp.where(lane < MLA_NOPE, 0.0, cos)

    qa = _dot(_bf(_rms(cq, qn_ref[...])), wq_ref[...])
    kv = _dot(_bf(_rms(ckv, kvn_ref[...])), wkv_ref[...])
    k_pe = kpe * cos_pe + kpe_rot * sin
    nslot = HEADS * SLOT
    for h in range(HEADS):
        sl = slice(h * SLOT, (h + 1) * SLOT)
        q_ref[:, sl] = _bf(qa[:, sl] * cos + qa[:, nslot + h * SLOT:nslot + (h + 1) * SLOT] * sin)
        k_ref[:, sl] = _bf(kv[:, sl] + k_pe)
    v_ref[...] = _bf(kv[:, nslot:nslot + BRANCH_WIDTH])


def _mla_prep(gb, qn, kvn, wq, wkv, cos_t, sin_t, seq, tm=512):
    t = gb.shape[0]
    tps = seq // tm
    row = lambda w: pl.BlockSpec((tm, w), lambda i: (i, 0))
    pos = pl.BlockSpec((tm, SLOT), lambda i: (i % tps, 0))
    nslot = HEADS * SLOT
    return pl.pallas_call(
        _mla_prep_kernel,
        grid=(t // tm,),
        in_specs=[row(B_WIDTH), _const_spec((1, MLA_Q_LORA)), _const_spec((1, MLA_KV_LORA)),
                  _const_spec((MLA_Q_LORA, 2 * nslot)), _const_spec((MLA_KV_LORA, nslot + BRANCH_WIDTH)),
                  pos, pos],
        out_specs=[row(nslot), row(nslot), row(BRANCH_WIDTH)],
        out_shape=[jax.ShapeDtypeStruct((t, nslot), BF16), jax.ShapeDtypeStruct((t, nslot), BF16),
                   jax.ShapeDtypeStruct((t, BRANCH_WIDTH), BF16)],
        compiler_params=_params("parallel"),
        name="mla_prep",
    )(gb, qn, kvn, wq, wkv, cos_t, sin_t)


def _fox_prep_kernel(d_ref, fb_ref, q_ref, k_ref, v_ref, carry_ref, *, tb, blocks_per_seq):
    i = pl.program_id(0)

    @pl.when(i % blocks_per_seq == 0)
    def _():
        carry_ref[...] = jnp.zeros_like(carry_ref)

    bw = BRANCH_WIDTH
    log_f = _log_sigmoid(d_ref[:, 3 * bw:3 * bw + LANE] + fb_ref[...])
    lmat = jnp.where(_tri(tb), 1.0, 0.0).astype(F32)
    fcum = _dot_hi(lmat, log_f) + carry_ref[...]
    carry_ref[...] = fcum[tb - 1:tb, :]
    f1 = _bf(fcum).astype(F32)
    f2 = _bf(fcum - f1).astype(F32)
    f3 = _bf(fcum - f1 - f2).astype(F32)

    lane = lax.broadcasted_iota(jnp.int32, (tb, LANE), 1)
    scale = HEAD_DIM ** -0.5
    for h in range(HEADS):
        pair, odd = divmod(h, 2)
        qp = d_ref[:, pair * LANE:(pair + 1) * LANE] * scale
        kp = d_ref[:, bw + pair * LANE:bw + (pair + 1) * LANE]
        head_lanes = (lane >= HEAD_DIM) if odd else (lane < HEAD_DIM)
        a0 = 0 if odd else HEAD_DIM
        fb = [jnp.broadcast_to(f[:, h:h + 1], (tb, LANE)) for f in (f1, f2, f3)]
        qx = jnp.zeros((tb, LANE), F32)
        kx = jnp.zeros((tb, LANE), F32)
        for j in range(3):
            qx = jnp.where(lane == a0 + j, fb[j], qx)
            qx = jnp.where(lane == a0 + 3 + j, 1.0, qx)
            kx = jnp.where(lane == a0 + j, 1.0, kx)
            kx = jnp.where(lane == a0 + 3 + j, -fb[j], kx)
        q_ref[:, h * SLOT:(h + 1) * SLOT] = _bf(jnp.where(head_lanes, qp, qx))
        k_ref[:, h * SLOT:(h + 1) * SLOT] = _bf(jnp.where(head_lanes, kp, kx))
    v_ref[...] = _bf(d_ref[:, 2 * bw:3 * bw])


def _fox_prep(gd, fbias, seq, tb=256):
    t = gd.shape[0]
    nslot = HEADS * SLOT
    row = lambda w: pl.BlockSpec((tb, w), lambda i: (i, 0))
    return pl.pallas_call(
        functools.partial(_fox_prep_kernel, tb=tb, blocks_per_seq=seq // tb),
        grid=(t // tb,),
        in_specs=[row(D_WIDTH), _const_spec((1, LANE))],
        out_specs=[row(nslot), row(nslot), row(BRANCH_WIDTH)],
        out_shape=[jax.ShapeDtypeStruct((t, nslot), BF16), jax.ShapeDtypeStruct((t, nslot), BF16),
                   jax.ShapeDtypeStruct((t, BRANCH_WIDTH), BF16)],
        scratch_shapes=[pltpu.VMEM((1, LANE), F32)],
        compiler_params=_params("arbitrary"),
        name="fox_prep",
    )(gd, fbias)


def _flash_kernel(q_ref, k_ref, v_ref, o_ref, m_ref, l_ref, acc_ref, *, scale, chunk_mask, tq, nk):
    qi = pl.program_id(2)
    ki = pl.program_id(3)

    @pl.when(ki == 0)
    def _():
        m_ref[...] = jnp.full_like(m_ref, -jnp.inf)
        l_ref[...] = jnp.zeros_like(l_ref)
        acc_ref[...] = jnp.zeros_like(acc_ref)

    def step(masked):
        v = v_ref[...]
        for hh in range(2):
            q = q_ref[:, hh * SLOT:(hh + 1) * SLOT]
            k = k_ref[:, hh * SLOT:(hh + 1) * SLOT]
            s = _dot_nt(q, k)
            if scale != 1.0:
                s = s * scale
            if masked:
                r = lax.broadcasted_iota(jnp.int32, s.shape, 0)
                c = lax.broadcasted_iota(jnp.int32, s.shape, 1)
                if chunk_mask:
                    ok = (c // CHUNK) <= (r // CHUNK)
                else:
                    ok = c <= r
                s = jnp.where(ok, s, -jnp.inf)
            m_prev = m_ref[hh][:, 0:1]
            m_new = jnp.maximum(m_prev, jnp.max(s, axis=-1, keepdims=True))
            alpha = jnp.exp(m_prev - m_new)
            p = jnp.exp(s - m_new)
            l_new = alpha * l_ref[hh][:, 0:1] + jnp.sum(p, axis=-1, keepdims=True)
            acc_ref[hh] = alpha * acc_ref[hh] + _dot(_bf(p), v)
            m_ref[hh] = jnp.broadcast_to(m_new, (tq, LANE))
            l_ref[hh] = jnp.broadcast_to(l_new, (tq, LANE))

    @pl.when(ki < qi)
    def _():
        step(False)

    @pl.when(ki == qi)
    def _():
        step(True)

    @pl.when(ki == nk - 1)
    def _():
        lane = lax.broadcasted_iota(jnp.int32, (tq, LANE), 1)
        o0 = acc_ref[0] / l_ref[0]
        o1 = acc_ref[1] / l_ref[1]
        o_ref[...] = _bf(jnp.where(lane < HEAD_DIM, o0, o1))


def _flash(q, k, v, batch, seq, scale, chunk_mask, tq=256):
    nq = seq // tq
    t = q.shape[0]
    kv_idx = lambda b, p, qi, ki: (b * nq + jnp.minimum(ki, qi), p)
    return pl.pallas_call(
        functools.partial(_flash_kernel, scale=scale, chunk_mask=chunk_mask, tq=tq, nk=nq),
        grid=(batch, HEADS // 2, nq, nq),
        in_specs=[pl.BlockSpec((tq, 2 * SLOT), lambda b, p, qi, ki: (b * nq + qi, p)),
                  pl.BlockSpec((tq, 2 * SLOT), kv_idx),
                  pl.BlockSpec((tq, 2 * HEAD_DIM), kv_idx)],
        out_specs=pl.BlockSpec((tq, 2 * HEAD_DIM), lambda b, p, qi, ki: (b * nq + qi, p)),
        out_shape=jax.ShapeDtypeStruct((t, BRANCH_WIDTH), BF16),
        scratch_shapes=[pltpu.VMEM((2, tq, LANE), F32), pltpu.VMEM((2, tq, LANE), F32),
                        pltpu.VMEM((2, tq, LANE), F32)],
        compiler_params=_params("parallel", "parallel", "parallel", "arbitrary"),
        name="flash_chunk" if chunk_mask else "flash_causal",
    )(q, k, v)


def _merge_kernel(x_ref, oa_ref, ob_ref, oc_ref, od_ref, gpre_ref, wg_ref, bg_ref, wup_ref,
                  wout_ref, gpost_ref, o_ref):
    x = x_ref[...]
    h = _bf(_rms(x, gpre_ref[...]))
    merged = jnp.zeros(x.shape, F32)
    for i, br in enumerate((oa_ref, ob_ref, oc_ref, od_ref)):
        gate = jax.nn.sigmoid(_dot(h, wg_ref[i]) + bg_ref[i:i + 1, :])
        merged = merged + gate * _dot(br[...], wup_ref[i])
    y = _dot(_bf(merged), wout_ref[...])
    o_ref[...] = x + _rms(y, gpost_ref[...])


def _merge(x2, oa, ob, oc, od, gpre, wg, bg, wup, wout, gpost, tm=512):
    t = x2.shape[0]
    row = lambda w: pl.BlockSpec((tm, w), lambda i: (i, 0))
    return pl.pallas_call(
        _merge_kernel,
        grid=(t // tm,),
        in_specs=[row(D_MODEL)] + [row(BRANCH_WIDTH)] * 4 +
                 [_const_spec((1, D_MODEL)), _const_spec((4, D_MODEL, D_MODEL)),
                  _const_spec((4, D_MODEL)), _const_spec((4, BRANCH_WIDTH, D_MODEL)),
                  _const_spec((D_MODEL, D_MODEL)), _const_spec((1, D_MODEL))],
        out_specs=row(D_MODEL),
        out_shape=jax.ShapeDtypeStruct((t, D_MODEL), F32),
        compiler_params=_params("parallel"),
        name="merge",
    )(x2, oa, ob, oc, od, gpre, wg, bg, wup, wout, gpost)


def _mlp_kernel(x_ref, gpre_ref, w1_ref, w2_ref, gpost_ref, o_ref, *, ff_chunk):
    x = x_ref[...]
    h = _bf(_rms(x, gpre_ref[...]))
    y = jnp.zeros(x.shape, F32)
    for c in range(D_FF // ff_chunk):
        u = jnp.maximum(_dot(h, w1_ref[:, c * ff_chunk:(c + 1) * ff_chunk]), 0.0)
        y = y + _dot(_bf(u * u), w2_ref[c * ff_chunk:(c + 1) * ff_chunk, :])
    o_ref[...] = x + _rms(y, gpost_ref[...])


def _mlp(x2, gpre, w1, w2, gpost, tm=512, ff_chunk=1024):
    t = x2.shape[0]
    row = pl.BlockSpec((tm, D_MODEL), lambda i: (i, 0))
    return pl.pallas_call(
        functools.partial(_mlp_kernel, ff_chunk=ff_chunk),
        grid=(t // tm,),
        in_specs=[row, _const_spec((1, D_MODEL)), _const_spec((D_MODEL, D_FF)),
                  _const_spec((D_FF, D_MODEL)), _const_spec((1, D_MODEL))],
        out_specs=row,
        out_shape=jax.ShapeDtypeStruct((t, D_MODEL), F32),
        compiler_params=_params("parallel"),
        name="mlp",
    )(x2, gpre, w1, w2, gpost)


_IN_SIZES = (
    HEADS * 3 * HEAD_DIM, HEADS * HEAD_DIM, HEADS, HEADS,
    MLA_Q_LORA, MLA_KV_LORA, MLA_ROPE,
    HEADS * GLA_DK, HEADS * GLA_DK, HEADS * HEAD_DIM, HEADS * HEAD_DIM, GLA_RANK,
    HEADS * HEAD_DIM, HEADS * HEAD_DIM, HEADS * HEAD_DIM, HEADS,
)


def _rot_cols(w):
    half = w.shape[-1] // 2
    return jnp.concatenate([-w[..., half:], w[..., :half]], axis=-1)


def _pad_cols(w, width):
    return jnp.pad(w, ((0, 0), (0, width - w.shape[-1])))


def _layout_in_weight(w_in):
    splits = np.cumsum(_IN_SIZES)[:-1].tolist()
    (dn_qkv, dn_z, dn_b, dn_a, cq, ckv, kpe, gq, gk, gv, gg, glr, fq, fk, fv, ff) = jnp.split(w_in, splits, axis=1)
    zeros = lambda n: jnp.zeros((w_in.shape[0], n), w_in.dtype)
    pe_slot = lambda w: jnp.concatenate([zeros(MLA_NOPE), w, zeros(SLOT - MLA_NOPE - MLA_ROPE)], axis=1)
    cols = [dn_qkv, dn_z, jnp.repeat(dn_a, HEAD_DIM, axis=1), jnp.repeat(dn_b, HEAD_DIM, axis=1),
            cq, ckv, pe_slot(kpe), pe_slot(_rot_cols(kpe)),
            gq, gk, gv, gg, _pad_cols(glr, LANE),
            fq, fk, fv, _pad_cols(ff, LANE)]
    w = jnp.concatenate(cols, axis=1)
    assert w.shape[1] == IN_TOTAL
    return _bf(w)


def _layout_q_up(w):
    zeros = lambda n: jnp.zeros((w.shape[0], n), w.dtype)
    per = MLA_NOPE + MLA_ROPE
    plain, rot = [], []
    for h in range(HEADS):
        nope = w[:, h * per:h * per + MLA_NOPE]
        pe = w[:, h * per + MLA_NOPE:(h + 1) * per]
        plain += [nope, pe, zeros(SLOT - per)]
        rot += [zeros(MLA_NOPE), _rot_cols(pe), zeros(SLOT - per)]
    return _bf(jnp.concatenate(plain + rot, axis=1))


def _layout_kv_up(w):
    zeros = lambda n: jnp.zeros((w.shape[0], n), w.dtype)
    per = MLA_NOPE + HEAD_DIM
    ks, vs = [], []
    for h in range(HEADS):
        ks += [w[:, h * per:h * per + MLA_NOPE], zeros(SLOT - MLA_NOPE)]
        vs += [w[:, h * per + MLA_NOPE:(h + 1) * per]]
    return _bf(jnp.concatenate(ks + vs, axis=1))


def _rope_tables(seq):
    inv_freq = 1.0 / (ROPE_THETA ** (jnp.arange(0, MLA_ROPE, 2, dtype=F32) / MLA_ROPE))
    ang = jnp.arange(seq, dtype=F32)[:, None] * inv_freq[None, :]
    ang = jnp.concatenate([ang, ang], axis=-1)
    pad = jnp.zeros((seq, SLOT - MLA_NOPE - MLA_ROPE), F32)
    cos_t = jnp.concatenate([jnp.ones((seq, MLA_NOPE), F32), jnp.cos(ang), pad], axis=1)
    sin_t = jnp.concatenate([jnp.zeros((seq, MLA_NOPE), F32), jnp.sin(ang), pad], axis=1)
    return cos_t, sin_t


def _layer(x2, batch, seq, p):
    ga, gb, gc, gd = _inproj(x2, p["norm_mix_pre"], p["w_cat"], p["dn_conv"], p["wgu"], p["gbias"], seq)
    o_a = _deltanet(ga, p["alog"], p["dtb"], p["dn_norm"], seq)
    q_b, k_b, v_b = _mla_prep(gb, p["mla_q_norm"], p["mla_kv_norm"], p["wq"], p["wkv"],
                              p["cos"], p["sin"], seq)
    o_b = _flash(q_b, k_b, v_b, batch, seq, (MLA_NOPE + MLA_ROPE) ** -0.5, True)
    o_c = _gla(gc, p["gla_norm"], seq)
    q_d, k_d, v_d = _fox_prep(gd, p["fbias"], seq)
    o_d = _flash(q_d, k_d, v_d, batch, seq, 1.0, False)
    x2 = _merge(x2, o_a, o_b, o_c, o_d, p["norm_mix_pre"], p["w_gate"], p["b_gate"],
                p["w_branch_up"], p["w_out"], p["norm_mix_post"])
    return _mlp(x2, p["norm_mlp_pre"], p["w_mlp_in"], p["w_mlp_out"], p["norm_mlp_post"])


def kernel(x, norm_mix_pre, norm_mix_post, norm_mlp_pre, norm_mlp_post, w_in, dn_conv, dn_a_log, dn_dt_bias, dn_norm, mla_q_norm, mla_w_q_up, mla_kv_norm, mla_w_kv_up, gla_w_gate_up, gla_gate_bias, gla_norm, fox_f_bias, w_branch_up, w_gate, b_gate, w_out, w_mlp_in, w_mlp_out):
    batch, seq, d = x.shape
    depth = w_in.shape[0]
    cos_t, sin_t = _rope_tables(seq)
    x2 = x.reshape(batch * seq, d)
    row = lambda a: a.reshape(1, -1)
    for l in range(depth):
        p = {
            "norm_mix_pre": row(norm_mix_pre[l]), "norm_mix_post": row(norm_mix_post[l]),
            "norm_mlp_pre": row(norm_mlp_pre[l]), "norm_mlp_post": row(norm_mlp_post[l]),
            "w_cat": _layout_in_weight(w_in[l]),
            "dn_conv": dn_conv[l],
            "wgu": _bf(jnp.pad(gla_w_gate_up[l], ((0, LANE - GLA_RANK), (0, 0)))),
            "gbias": row(gla_gate_bias[l]),
            "alog": row(jnp.repeat(dn_a_log[l], HEAD_DIM)), "dtb": row(jnp.repeat(dn_dt_bias[l], HEAD_DIM)),
            "dn_norm": row(dn_norm[l]),
            "mla_q_norm": row(mla_q_norm[l]), "mla_kv_norm": row(mla_kv_norm[l]),
            "wq": _layout_q_up(mla_w_q_up[l]), "wkv": _layout_kv_up(mla_w_kv_up[l]),
            "cos": cos_t, "sin": sin_t,
            "gla_norm": row(gla_norm[l]),
            "fbias": row(jnp.pad(fox_f_bias[l], (0, LANE - HEADS))),
            "w_gate": _bf(w_gate[l]), "b_gate": b_gate[l], "w_branch_up": _bf(w_branch_up[l]),
            "w_out": _bf(w_out[l]), "w_mlp_in": _bf(w_mlp_in[l]), "w_mlp_out": _bf(w_mlp_out[l]),
        }
        x2 = _layer(x2, batch, seq, p)
    return x2.reshape(batch, seq, d)
```

```python
import functools
import math

import jax
import jax.numpy as jnp
import numpy as np
from jax import lax
from jax.experimental import pallas as pl
from jax.experimental.pallas import tpu as pltpu

F32 = jnp.float32
BF16 = jnp.bfloat16

D_MODEL = 1024
CHUNK = 64
EPS = 1e-6
HEADS = 4
HEAD_DIM = 64
BRANCH_WIDTH = HEADS * HEAD_DIM
D_FF = 4 * D_MODEL
CONV_K = 4
MLA_NOPE = HEAD_DIM
MLA_ROPE = HEAD_DIM // 2
MLA_Q_LORA = D_MODEL // 4
MLA_KV_LORA = D_MODEL // 8
ROPE_THETA = 10000.0
GLA_DK = HEAD_DIM // 2
GLA_RANK = 16
GLA_TAU = 16.0
LANE = 128
SLOT = LANE

A_WIDTH = 3 * BRANCH_WIDTH + 3 * BRANCH_WIDTH
B_WIDTH = MLA_Q_LORA + MLA_KV_LORA + 2 * SLOT
C_WIDTH = 2 * HEADS * GLA_DK + 2 * BRANCH_WIDTH + LANE
D_WIDTH = 3 * BRANCH_WIDTH + LANE
IN_TOTAL = A_WIDTH + B_WIDTH + C_WIDTH + D_WIDTH

VMEM_LIMIT = 56 * 1024 * 1024


def _dot(a, b):
    return jnp.dot(a, b, preferred_element_type=F32)


def _dot_nt(a, b):
    return lax.dot_general(a, b, (((1,), (1,)), ((), ())), preferred_element_type=F32)


def _dot_tn(a, b):
    return lax.dot_general(a, b, (((0,), (0,)), ((), ())), preferred_element_type=F32)


def _bf(x):
    return x.astype(BF16)


def _split(x, pieces):
    out = []
    for _ in range(pieces - 1):
        hi = _bf(x)
        out.append(hi)
        x = x - hi.astype(F32)
    out.append(_bf(x))
    return out


def _dot3(a, b):
    ah, al = _split(a, 2)
    bh, bl = _split(b, 2)
    return _dot(ah, bh) + (_dot(ah, bl) + _dot(al, bh))


def _dot_ones(ones_bf, x):
    x1, x2, x3 = _split(x, 3)
    return _dot(ones_bf, x1) + (_dot(ones_bf, x2) + _dot(ones_bf, x3))


def _rms(x, gain):
    return x * lax.rsqrt(jnp.mean(x * x, axis=-1, keepdims=True) + EPS) * gain


def _silu(x):
    return x * jax.nn.sigmoid(x)


def _softplus(x):
    return jnp.maximum(x, 0.0) + jnp.log1p(jnp.exp(-jnp.abs(x)))


def _log_sigmoid(x):
    return jnp.minimum(x, 0.0) - jnp.log1p(jnp.exp(-jnp.abs(x)))


def _tri(n, strict=False):
    r = lax.broadcasted_iota(jnp.int32, (n, n), 0)
    c = lax.broadcasted_iota(jnp.int32, (n, n), 1)
    return (c < r) if strict else (c <= r)


def _params(*sem):
    return pltpu.CompilerParams(dimension_semantics=sem, vmem_limit_bytes=VMEM_LIMIT)


def _const_spec(shape):
    nd = len(shape)
    return pl.BlockSpec(shape, lambda *_: (0,) * nd, pipeline_mode=pl.Buffered(1))


def _inproj_kernel(x_ref, gain_ref, w_ref, convw_ref, wgu_ref, gbias_ref,
                   oa_ref, ob_ref, oc_ref, od_ref, xe_ref, *, tm, tiles_per_seq):
    i = pl.program_id(0)
    h = _bf(_rms(x_ref[...], gain_ref[...]))

    qkv_w = 3 * BRANCH_WIDTH
    pa = _dot(h, w_ref[:, 0:A_WIDTH])
    raw = pa[:, 0:qkv_w]

    @pl.when(i % tiles_per_seq == 0)
    def _():
        xe_ref[0:8, :] = jnp.zeros((8, qkv_w), F32)

    xe_ref[8:8 + tm, :] = raw
    y = jnp.zeros((tm, qkv_w), F32)
    for j in range(CONV_K):
        y = y + convw_ref[j:j + 1, :] * xe_ref[pl.ds(8 - (CONV_K - 1) + j, tm), :]
    tail = xe_ref[tm:tm + 8, :]
    xe_ref[0:8, :] = tail
    oa_ref[:, 0:qkv_w] = _silu(y)
    oa_ref[:, qkv_w:A_WIDTH] = pa[:, qkv_w:A_WIDTH]

    off = A_WIDTH
    ob_ref[...] = _dot(h, w_ref[:, off:off + B_WIDTH])
    off += B_WIDTH
    pc = _dot(h, w_ref[:, off:off + C_WIDTH])
    lr_off = C_WIDTH - LANE
    oc_ref[:, 0:lr_off] = pc[:, 0:lr_off]
    gate = _dot(_bf(pc[:, lr_off:C_WIDTH]), wgu_ref[...]) + gbias_ref[...]
    oc_ref[:, lr_off:C_WIDTH] = _log_sigmoid(gate) * (1.0 / GLA_TAU)
    off += C_WIDTH
    od_ref[...] = _dot(h, w_ref[:, off:off + D_WIDTH])


def _inproj(x2, gain, w_cat, convw, wgu, gbias, seq, tm=256):
    t = x2.shape[0]
    grid = (t // tm,)
    row = lambda w: pl.BlockSpec((tm, w), lambda i: (i, 0))
    return pl.pallas_call(
        functools.partial(_inproj_kernel, tm=tm, tiles_per_seq=seq // tm),
        grid=grid,
        in_specs=[row(D_MODEL), _const_spec((1, D_MODEL)), _const_spec((D_MODEL, IN_TOTAL)),
                  _const_spec((CONV_K, 3 * BRANCH_WIDTH)), _const_spec((LANE, LANE)),
                  _const_spec((1, LANE))],
        out_specs=[row(A_WIDTH), row(B_WIDTH), row(C_WIDTH), row(D_WIDTH)],
        out_shape=[jax.ShapeDtypeStruct((t, w), F32) for w in (A_WIDTH, B_WIDTH, C_WIDTH, D_WIDTH)],
        scratch_shapes=[pltpu.VMEM((tm + 8, 3 * BRANCH_WIDTH), F32)],
        compiler_params=_params("arbitrary"),
        name="inproj",
    )(x2, gain, w_cat, convw, wgu, gbias)


def _deltanet_kernel(a_ref, alog_ref, dtb_ref, norm_ref, o_ref, state_ref, *, tb, blocks_per_seq):
    i = pl.program_id(0)

    @pl.when(i % blocks_per_seq == 0)
    def _():
        state_ref[...] = jnp.zeros_like(state_ref)

    bw = BRANCH_WIDTH
    nchunk = tb // CHUNK
    lower = _tri(CHUNK)
    strict = _tri(CHUNK, True)
    lmat = _bf(jnp.where(lower, 1.0, 0.0))
    smat = jnp.where(strict, 1.0, 0.0).astype(F32)
    eye = jnp.where(lower & ~strict, 1.0, 0.0).astype(F32)
    neg_a = -jnp.exp(alog_ref[...])
    g_all = neg_a * _softplus(a_ref[:, 4 * bw:5 * bw] + dtb_ref[...])
    beta_all = jax.nn.sigmoid(a_ref[:, 5 * bw:6 * bw])

    idx = [(c, h) for c in range(nchunk) for h in range(HEADS)]

    def head_block(col0, c, h):
        return a_ref[c * CHUNK:(c + 1) * CHUNK, col0 + h * HEAD_DIM:col0 + (h + 1) * HEAD_DIM]

    def lanes(x, c, h):
        return x[c * CHUNK:(c + 1) * CHUNK, h * HEAD_DIM:(h + 1) * HEAD_DIM]

    qs, ks, vs = [], [], []
    for c, h in idx:
        q = head_block(0, c, h)
        k = head_block(bw, c, h)
        qs.append(q * lax.rsqrt(jnp.sum(q * q, axis=-1, keepdims=True) + EPS) * (HEAD_DIM ** -0.5))
        ks.append(k * lax.rsqrt(jnp.sum(k * k, axis=-1, keepdims=True) + EPS))
        vs.append(head_block(2 * bw, c, h))
    gs = [lanes(g_all, c, h) for c, h in idx]
    betas = [lanes(beta_all, c, h) for c, h in idx]

    gcs = [_dot_ones(lmat, g) for g in gs]
    diffs = [_dot_ones(lmat, g * smat) for g in gs]
    decays = [jnp.where(lower, jnp.exp(jnp.where(lower, d, 0.0)), 0.0) for d in diffs]
    kbs = [_bf(k) for k in ks]
    kks = [_dot_nt(kb, kb) for kb in kbs]
    pows = [jnp.where(strict, b * kk * d, 0.0) for b, kk, d in zip(betas, kks, decays)]
    tinvs = [eye - a for a in pows]
    for _ in range(CHUNK.bit_length() - 2):
        pows = [_dot3(p, p) for p in pows]
        tinvs = [t + _dot3(t, p) for t, p in zip(tinvs, pows)]
    egcs = [jnp.exp(gc) for gc in gcs]
    us = [_dot3(t, v * b) for t, v, b in zip(tinvs, vs, betas)]
    ws = [_dot3(t, k * b * e) for t, k, b, e in zip(tinvs, ks, betas, egcs)]
    qks = [_bf(_dot_nt(_bf(q), kb) * d) for q, kb, d in zip(qs, kbs, decays)]
    q_decays = [_bf(q * e) for q, e in zip(qs, egcs)]
    lasts = [gc[CHUNK - 1:CHUNK, :] for gc in gcs]
    k_tails = [_bf(k * jnp.exp(l - gc)) for k, l, gc in zip(ks, lasts, gcs)]
    cds = [jnp.exp(l) for l in lasts]
    wbs = [_bf(w) for w in ws]

    states = [state_ref[h] for h in range(HEADS)]
    for c in range(nchunk):
        outs = []
        for h in range(HEADS):
            n = c * HEADS + h
            s = states[h]
            sb = _bf(s)
            vb = _bf(us[n] - _dot(wbs[n], sb))
            o = _dot(q_decays[n], sb) + _dot(qks[n], vb)
            states[h] = s * cds[n] + _dot_tn(k_tails[n], vb)
            outs.append(_rms(o, norm_ref[...]) * _silu(head_block(3 * bw, c, h)))
        o_ref[c * CHUNK:(c + 1) * CHUNK, :] = _bf(jnp.concatenate(outs, axis=1))
    for h in range(HEADS):
        state_ref[h] = states[h]


def _deltanet(ga, alog, dtb, norm, seq, tb=256):
    t = ga.shape[0]
    return pl.pallas_call(
        functools.partial(_deltanet_kernel, tb=tb, blocks_per_seq=seq // tb),
        grid=(t // tb,),
        in_specs=[pl.BlockSpec((tb, A_WIDTH), lambda i: (i, 0)),
                  _const_spec((1, BRANCH_WIDTH)), _const_spec((1, BRANCH_WIDTH)),
                  _const_spec((1, HEAD_DIM))],
        out_specs=pl.BlockSpec((tb, BRANCH_WIDTH), lambda i: (i, 0)),
        out_shape=jax.ShapeDtypeStruct((t, BRANCH_WIDTH), BF16),
        scratch_shapes=[pltpu.VMEM((HEADS, HEAD_DIM, HEAD_DIM), F32)],
        compiler_params=_params("arbitrary"),
        name="deltanet",
    )(ga, alog, dtb, norm)


def _gla_kernel(c_ref, norm_ref, o_ref, state_ref, *, tb, blocks_per_seq):
    i = pl.program_id(0)

    @pl.when(i % blocks_per_seq == 0)
    def _():
        state_ref[...] = jnp.zeros_like(state_ref)

    qw = HEADS * GLA_DK
    bw = BRANCH_WIDTH
    nchunk = tb // CHUNK
    lower = _tri(CHUNK)
    lmat = _bf(jnp.where(lower, 1.0, 0.0))
    la_off = 2 * qw + 2 * bw
    idx = [(c, h) for c in range(nchunk) for h in range(HEADS)]

    def blk(col0, width, c, h):
        return c_ref[c * CHUNK:(c + 1) * CHUNK, col0 + h * width:col0 + (h + 1) * width]

    bcums = [_dot_ones(lmat, blk(la_off, GLA_DK, c, h)) for c, h in idx]
    q_ins = [_bf(blk(0, GLA_DK, c, h) * (GLA_DK ** -0.5) * jnp.exp(b)) for (c, h), b in zip(idx, bcums)]
    k_ins = [_bf(blk(qw, GLA_DK, c, h) * jnp.exp(-b)) for (c, h), b in zip(idx, bcums)]
    vbs = [_bf(blk(2 * qw, HEAD_DIM, c, h)) for c, h in idx]
    intras = [_dot(_bf(jnp.where(lower, _dot_nt(qi, ki), 0.0)), vb) for qi, ki, vb in zip(q_ins, k_ins, vbs)]
    lasts = [b[CHUNK - 1:CHUNK, :] for b in bcums]
    k_tails = [_bf(blk(qw, GLA_DK, c, h) * jnp.exp(l - b)) for (c, h), l, b in zip(idx, lasts, bcums)]
    cds = [jnp.exp(l) for l in lasts]

    states = [state_ref[h] for h in range(HEADS)]
    for c in range(nchunk):
        outs = []
        for h in range(HEADS):
            n = c * HEADS + h
            o = intras[n] + _dot_nt(q_ins[n], _bf(states[h]))
            states[h] = states[h] * cds[n] + _dot_tn(vbs[n], k_tails[n])
            outs.append(_rms(o, norm_ref[...]) * _silu(blk(2 * qw + bw, HEAD_DIM, c, h)))
        o_ref[c * CHUNK:(c + 1) * CHUNK, :] = _bf(jnp.concatenate(outs, axis=1))
    for h in range(HEADS):
        state_ref[h] = states[h]


def _gla(gc, norm, seq, tb=256):
    t = gc.shape[0]
    return pl.pallas_call(
        functools.partial(_gla_kernel, tb=tb, blocks_per_seq=seq // tb),
        grid=(t // tb,),
        in_specs=[pl.BlockSpec((tb, C_WIDTH), lambda i: (i, 0)), _const_spec((1, HEAD_DIM))],
        out_specs=pl.BlockSpec((tb, BRANCH_WIDTH), lambda i: (i, 0)),
        out_shape=jax.ShapeDtypeStruct((t, BRANCH_WIDTH), BF16),
        scratch_shapes=[pltpu.VMEM((HEADS, HEAD_DIM, GLA_DK), F32)],
        compiler_params=_params("arbitrary"),
        name="gla",
    )(gc, norm)


def _mla_prep_kernel(b_ref, qn_ref, kvn_ref, wq_ref, wkv_ref, cos_ref, sin_ref,
                     q_ref, k_ref, v_ref):
    cq = b_ref[:, 0:MLA_Q_LORA]
    ckv = b_ref[:, MLA_Q_LORA:MLA_Q_LORA + MLA_KV_LORA]
    off = MLA_Q_LORA + MLA_KV_LORA
    kpe = b_ref[:, off:off + SLOT]
    kpe_rot = b_ref[:, off + SLOT:off + 2 * SLOT]
    cos = cos_ref[...]
    sin = sin_ref[...]

    qa = _dot(_bf(_rms(cq, qn_ref[...])), wq_ref[...])
    kv = _dot(_bf(_rms(ckv, kvn_ref[...])), wkv_ref[...])
    k_pe = kpe * cos + kpe_rot * sin
    nslot = HEADS * SLOT
    for h in range(HEADS):
        sl = slice(h * SLOT, (h + 1) * SLOT)
        q_ref[:, sl] = _bf(qa[:, sl] * cos + qa[:, nslot + h * SLOT:nslot + (h + 1) * SLOT] * sin)
        k_ref[:, sl] = _bf(kv[:, sl] + k_pe)
    v_ref[...] = _bf(kv[:, nslot:nslot + BRANCH_WIDTH])


def _mla_prep(gb, qn, kvn, wq, wkv, cos_t, sin_t, seq, tm=512):
    t = gb.shape[0]
    tps = seq // tm
    row = lambda w: pl.BlockSpec((tm, w), lambda i: (i, 0))
    pos = pl.BlockSpec((tm, SLOT), lambda i: (i % tps, 0))
    nslot = HEADS * SLOT
    return pl.pallas_call(
        _mla_prep_kernel,
        grid=(t // tm,),
        in_specs=[row(B_WIDTH), _const_spec((1, MLA_Q_LORA)), _const_spec((1, MLA_KV_LORA)),
                  _const_spec((MLA_Q_LORA, 2 * nslot)), _const_spec((MLA_KV_LORA, nslot + BRANCH_WIDTH)),
                  pos, pos],
        out_specs=[row(nslot), row(nslot), row(BRANCH_WIDTH)],
        out_shape=[jax.ShapeDtypeStruct((t, nslot), BF16), jax.ShapeDtypeStruct((t, nslot), BF16),
                   jax.ShapeDtypeStruct((t, BRANCH_WIDTH), BF16)],
        compiler_params=_params("parallel"),
        name="mla_prep",
    )(gb, qn, kvn, wq, wkv, cos_t, sin_t)


def _fox_prep_kernel(d_ref, fb_ref, q_ref, k_ref, v_ref, carry_ref, *, tb, blocks_per_seq):
    i = pl.program_id(0)

    @pl.when(i % blocks_per_seq == 0)
    def _():
        carry_ref[...] = jnp.zeros_like(carry_ref)

    bw = BRANCH_WIDTH
    log_f = _log_sigmoid(d_ref[:, 3 * bw:3 * bw + LANE] + fb_ref[...])
    lmat = _bf(jnp.where(_tri(tb), 1.0, 0.0))
    fcum = _dot_ones(lmat, log_f) + carry_ref[...]
    carry_ref[...] = fcum[tb - 1:tb, :]
    f1 = _bf(fcum).astype(F32)
    f2 = _bf(fcum - f1).astype(F32)
    f3 = _bf(fcum - f1 - f2).astype(F32)

    lane = lax.broadcasted_iota(jnp.int32, (tb, LANE), 1)
    scale = HEAD_DIM ** -0.5
    for h in range(HEADS):
        pair, odd = divmod(h, 2)
        qp = d_ref[:, pair * LANE:(pair + 1) * LANE] * scale
        kp = d_ref[:, bw + pair * LANE:bw + (pair + 1) * LANE]
        head_lanes = (lane >= HEAD_DIM) if odd else (lane < HEAD_DIM)
        a0 = 0 if odd else HEAD_DIM
        fb = [jnp.broadcast_to(f[:, h:h + 1], (tb, LANE)) for f in (f1, f2, f3)]
        qx = jnp.zeros((tb, LANE), F32)
        kx = jnp.zeros((tb, LANE), F32)
        for j in range(3):
            qx = jnp.where(lane == a0 + j, fb[j], qx)
            qx = jnp.where(lane == a0 + 3 + j, 1.0, qx)
            kx = jnp.where(lane == a0 + j, 1.0, kx)
            kx = jnp.where(lane == a0 + 3 + j, -fb[j], kx)
        q_ref[:, h * SLOT:(h + 1) * SLOT] = _bf(jnp.where(head_lanes, qp, qx))
        k_ref[:, h * SLOT:(h + 1) * SLOT] = _bf(jnp.where(head_lanes, kp, kx))
    v_ref[...] = _bf(d_ref[:, 2 * bw:3 * bw])


def _fox_prep(gd, fbias, seq, tb=256):
    t = gd.shape[0]
    nslot = HEADS * SLOT
    row = lambda w: pl.BlockSpec((tb, w), lambda i: (i, 0))
    return pl.pallas_call(
        functools.partial(_fox_prep_kernel, tb=tb, blocks_per_seq=seq // tb),
        grid=(t // tb,),
        in_specs=[row(D_WIDTH), _const_spec((1, LANE))],
        out_specs=[row(nslot), row(nslot), row(BRANCH_WIDTH)],
        out_shape=[jax.ShapeDtypeStruct((t, nslot), BF16), jax.ShapeDtypeStruct((t, nslot), BF16),
                   jax.ShapeDtypeStruct((t, BRANCH_WIDTH), BF16)],
        scratch_shapes=[pltpu.VMEM((1, LANE), F32)],
        compiler_params=_params("arbitrary"),
        name="fox_prep",
    )(gd, fbias)


def _flash_kernel(q_ref, k_ref, vt_ref, o_ref, m_ref, l_ref, acc_ref, *, scale, chunk_mask, tq):
    qi = pl.program_id(1)
    m_ref[...] = jnp.full_like(m_ref, -jnp.inf)
    l_ref[...] = jnp.zeros_like(l_ref)
    acc_ref[...] = jnp.zeros_like(acc_ref)
    c2 = scale * math.log2(math.e)

    def tile(ki, masked):
        heads = range(HEADS)
        ss = [_dot_nt(k_ref[ki, :, hh * SLOT:(hh + 1) * SLOT], q_ref[:, hh * SLOT:(hh + 1) * SLOT])
              for hh in heads]
        if masked:
            kpos = lax.broadcasted_iota(jnp.int32, ss[0].shape, 0)
            qpos = lax.broadcasted_iota(jnp.int32, ss[0].shape, 1)
            if chunk_mask:
                shift = CHUNK.bit_length() - 1
                ok = (kpos >> shift) <= (qpos >> shift)
            else:
                ok = kpos <= qpos
            ss = [jnp.where(ok, s, -jnp.inf) for s in ss]
        m_prevs = [m_ref[hh] for hh in heads]
        m_news = [jnp.maximum(mp, jnp.max(s, axis=0, keepdims=True)) for mp, s in zip(m_prevs, ss)]
        alphas = [jnp.exp2((mp - mn) * c2) for mp, mn in zip(m_prevs, m_news)]
        ps = [jnp.exp2((s - mn) * c2) for s, mn in zip(ss, m_news)]
        pvs = [_dot(vt_ref[ki, hh * HEAD_DIM:(hh + 1) * HEAD_DIM, :], _bf(ps[hh])) for hh in heads]
        for hh in heads:
            l_ref[hh] = alphas[hh] * l_ref[hh] + jnp.sum(ps[hh], axis=0, keepdims=True)
            acc_ref[hh] = alphas[hh] * acc_ref[hh] + pvs[hh]
            m_ref[hh] = m_news[hh]

    def body(ki, carry):
        tile(ki, False)
        return carry

    lax.fori_loop(0, qi, body, 0)
    tile(qi, True)
    ot = jnp.concatenate([acc_ref[hh] / l_ref[hh] for hh in range(HEADS)], axis=0)
    o_ref[...] = _bf(ot.T)


def _flash(q, k, v, batch, seq, scale, chunk_mask, tq=512):
    nq = seq // tq
    t = q.shape[0]
    k3 = k.reshape(t // tq, tq, HEADS * SLOT)
    vt3 = v.reshape(t // tq, tq, BRANCH_WIDTH).transpose(0, 2, 1)
    return pl.pallas_call(
        functools.partial(_flash_kernel, scale=scale, chunk_mask=chunk_mask, tq=tq),
        grid=(batch, nq),
        in_specs=[pl.BlockSpec((tq, HEADS * SLOT), lambda b, qi: (b * nq + qi, 0)),
                  pl.BlockSpec((nq, tq, HEADS * SLOT), lambda b, qi: (b, 0, 0)),
                  pl.BlockSpec((nq, BRANCH_WIDTH, tq), lambda b, qi: (b, 0, 0))],
        out_specs=pl.BlockSpec((tq, BRANCH_WIDTH), lambda b, qi: (b * nq + qi, 0)),
        out_shape=jax.ShapeDtypeStruct((t, BRANCH_WIDTH), BF16),
        scratch_shapes=[pltpu.VMEM((HEADS, 1, tq), F32), pltpu.VMEM((HEADS, 1, tq), F32),
                        pltpu.VMEM((HEADS, HEAD_DIM, tq), F32)],
        compiler_params=_params("parallel", "arbitrary"),
        name="flash_chunk" if chunk_mask else "flash_causal",
    )(q, k3, vt3)


def _merge_kernel(x_ref, oa_ref, ob_ref, oc_ref, od_ref, gpre_ref, wg_ref, bg_ref, wup_ref,
                  wout_ref, gpost_ref, o_ref):
    x = x_ref[...]
    h = _bf(_rms(x, gpre_ref[...]))
    merged = jnp.zeros(x.shape, F32)
    for i, br in enumerate((oa_ref, ob_ref, oc_ref, od_ref)):
        gate = jax.nn.sigmoid(_dot(h, wg_ref[i]) + bg_ref[i:i + 1, :])
        merged = merged + gate * _dot(br[...], wup_ref[i])
    y = _dot(_bf(merged), wout_ref[...])
    o_ref[...] = x + _rms(y, gpost_ref[...])


def _merge(x2, oa, ob, oc, od, gpre, wg, bg, wup, wout, gpost, tm=512):
    t = x2.shape[0]
    row = lambda w: pl.BlockSpec((tm, w), lambda i: (i, 0))
    return pl.pallas_call(
        _merge_kernel,
        grid=(t // tm,),
        in_specs=[row(D_MODEL)] + [row(BRANCH_WIDTH)] * 4 +
                 [_const_spec((1, D_MODEL)), _const_spec((4, D_MODEL, D_MODEL)),
                  _const_spec((4, D_MODEL)), _const_spec((4, BRANCH_WIDTH, D_MODEL)),
                  _const_spec((D_MODEL, D_MODEL)), _const_spec((1, D_MODEL))],
        out_specs=row(D_MODEL),
        out_shape=jax.ShapeDtypeStruct((t, D_MODEL), F32),
        compiler_params=_params("parallel"),
        name="merge",
    )(x2, oa, ob, oc, od, gpre, wg, bg, wup, wout, gpost)


def _mlp_kernel(x_ref, gpre_ref, w1_ref, w2_ref, gpost_ref, o_ref, *, ff_chunk):
    x = x_ref[...]
    h = _bf(_rms(x, gpre_ref[...]))
    y = jnp.zeros(x.shape, F32)
    for c in range(D_FF // ff_chunk):
        u = jnp.maximum(_dot(h, w1_ref[:, c * ff_chunk:(c + 1) * ff_chunk]), 0.0)
        y = y + _dot(_bf(u * u), w2_ref[c * ff_chunk:(c + 1) * ff_chunk, :])
    o_ref[...] = x + _rms(y, gpost_ref[...])


def _mlp(x2, gpre, w1, w2, gpost, tm=512, ff_chunk=1024):
    t = x2.shape[0]
    row = pl.BlockSpec((tm, D_MODEL), lambda i: (i, 0))
    return pl.pallas_call(
        functools.partial(_mlp_kernel, ff_chunk=ff_chunk),
        grid=(t // tm,),
        in_specs=[row, _const_spec((1, D_MODEL)), _const_spec((D_MODEL, D_FF)),
                  _const_spec((D_FF, D_MODEL)), _const_spec((1, D_MODEL))],
        out_specs=row,
        out_shape=jax.ShapeDtypeStruct((t, D_MODEL), F32),
        compiler_params=_params("parallel"),
        name="mlp",
    )(x2, gpre, w1, w2, gpost)


_IN_SIZES = (
    HEADS * 3 * HEAD_DIM, HEADS * HEAD_DIM, HEADS, HEADS,
    MLA_Q_LORA, MLA_KV_LORA, MLA_ROPE,
    HEADS * GLA_DK, HEADS * GLA_DK, HEADS * HEAD_DIM, HEADS * HEAD_DIM, GLA_RANK,
    HEADS * HEAD_DIM, HEADS * HEAD_DIM, HEADS * HEAD_DIM, HEADS,
)


def _rot_cols(w):
    half = w.shape[-1] // 2
    return jnp.concatenate([-w[..., half:], w[..., :half]], axis=-1)


def _pad_cols(w, width):
    return jnp.pad(w, ((0, 0), (0, width - w.shape[-1])))


def _layout_in_weight(w_in):
    splits = np.cumsum(_IN_SIZES)[:-1].tolist()
    (dn_qkv, dn_z, dn_b, dn_a, cq, ckv, kpe, gq, gk, gv, gg, glr, fq, fk, fv, ff) = jnp.split(w_in, splits, axis=1)
    zeros = lambda n: jnp.zeros((w_in.shape[0], n), w_in.dtype)
    pe_slot = lambda w: jnp.concatenate([zeros(MLA_NOPE), w, zeros(SLOT - MLA_NOPE - MLA_ROPE)], axis=1)
    cols = [dn_qkv, dn_z, jnp.repeat(dn_a, HEAD_DIM, axis=1), jnp.repeat(dn_b, HEAD_DIM, axis=1),
            cq, ckv, pe_slot(kpe), pe_slot(_rot_cols(kpe)),
            gq, gk, gv, gg, _pad_cols(glr, LANE),
            fq, fk, fv, _pad_cols(ff, LANE)]
    w = jnp.concatenate(cols, axis=1)
    assert w.shape[1] == IN_TOTAL
    return _bf(w)


def _layout_q_up(w):
    zeros = lambda n: jnp.zeros((w.shape[0], n), w.dtype)
    per = MLA_NOPE + MLA_ROPE
    plain, rot = [], []
    for h in range(HEADS):
        nope = w[:, h * per:h * per + MLA_NOPE]
        pe = w[:, h * per + MLA_NOPE:(h + 1) * per]
        plain += [nope, pe, zeros(SLOT - per)]
        rot += [zeros(MLA_NOPE), _rot_cols(pe), zeros(SLOT - per)]
    return _bf(jnp.concatenate(plain + rot, axis=1))


def _layout_kv_up(w):
    zeros = lambda n: jnp.zeros((w.shape[0], n), w.dtype)
    per = MLA_NOPE + HEAD_DIM
    ks, vs = [], []
    for h in range(HEADS):
        ks += [w[:, h * per:h * per + MLA_NOPE], zeros(SLOT - MLA_NOPE)]
        vs += [w[:, h * per + MLA_NOPE:(h + 1) * per]]
    return _bf(jnp.concatenate(ks + vs, axis=1))


def _rope_tables(seq):
    inv_freq = 1.0 / (ROPE_THETA ** (jnp.arange(0, MLA_ROPE, 2, dtype=F32) / MLA_ROPE))
    ang = jnp.arange(seq, dtype=F32)[:, None] * inv_freq[None, :]
    ang = jnp.concatenate([ang, ang], axis=-1)
    pad = jnp.zeros((seq, SLOT - MLA_NOPE - MLA_ROPE), F32)
    cos_t = jnp.concatenate([jnp.ones((seq, MLA_NOPE), F32), jnp.cos(ang), pad], axis=1)
    sin_t = jnp.concatenate([jnp.zeros((seq, MLA_NOPE), F32), jnp.sin(ang), pad], axis=1)
    return cos_t, sin_t


def _layer(x2, batch, seq, p):
    ga, gb, gc, gd = _inproj(x2, p["norm_mix_pre"], p["w_cat"], p["dn_conv"], p["wgu"], p["gbias"], seq)
    o_a = _deltanet(ga, p["alog"], p["dtb"], p["dn_norm"], seq)
    q_b, k_b, v_b = _mla_prep(gb, p["mla_q_norm"], p["mla_kv_norm"], p["wq"], p["wkv"],
                              p["cos"], p["sin"], seq)
    o_b = _flash(q_b, k_b, v_b, batch, seq, (MLA_NOPE + MLA_ROPE) ** -0.5, True)
    o_c = _gla(gc, p["gla_norm"], seq)
    q_d, k_d, v_d = _fox_prep(gd, p["fbias"], seq)
    o_d = _flash(q_d, k_d, v_d, batch, seq, 1.0, False)
    x2 = _merge(x2, o_a, o_b, o_c, o_d, p["norm_mix_pre"], p["w_gate"], p["b_gate"],
                p["w_branch_up"], p["w_out"], p["norm_mix_post"])
    return _mlp(x2, p["norm_mlp_pre"], p["w_mlp_in"], p["w_mlp_out"], p["norm_mlp_post"])


def kernel(x, norm_mix_pre, norm_mix_post, norm_mlp_pre, norm_mlp_post, w_in, dn_conv, dn_a_log, dn_dt_bias, dn_norm, mla_q_norm, mla_w_q_up, mla_kv_norm, mla_w_kv_up, gla_w_gate_up, gla_gate_bias, gla_norm, fox_f_bias, w_branch_up, w_gate, b_gate, w_out, w_mlp_in, w_mlp_out):
    batch, seq, d = x.shape
    depth = w_in.shape[0]
    cos_t, sin_t = _rope_tables(seq)
    x2 = x.reshape(batch * seq, d)
    row = lambda a: a.reshape(1, -1)
    for l in range(depth):
        p = {
            "norm_mix_pre": row(norm_mix_pre[l]), "norm_mix_post": row(norm_mix_post[l]),
            "norm_mlp_pre": row(norm_mlp_pre[l]), "norm_mlp_post": row(norm_mlp_post[l]),
            "w_cat": _layout_in_weight(w_in[l]),
            "dn_conv": dn_conv[l],
            "wgu": _bf(jnp.pad(gla_w_gate_up[l], ((0, LANE - GLA_RANK), (0, 0)))),
            "gbias": row(gla_gate_bias[l]),
            "alog": row(jnp.repeat(dn_a_log[l], HEAD_DIM)), "dtb": row(jnp.repeat(dn_dt_bias[l], HEAD_DIM)),
            "dn_norm": row(dn_norm[l]),
            "mla_q_norm": row(mla_q_norm[l]), "mla_kv_norm": row(mla_kv_norm[l]),
            "wq": _layout_q_up(mla_w_q_up[l]), "wkv": _layout_kv_up(mla_w_kv_up[l]),
            "cos": cos_t, "sin": sin_t,
            "gla_norm": row(gla_norm[l]),
            "fbias": row(jnp.pad(fox_f_bias[l], (0, LANE - HEADS))),
            "w_gate": _bf(w_gate[l]), "b_gate": b_gate[l], "w_branch_up": _bf(w_branch_up[l]),
            "w_out": _bf(w_out[l]), "w_mlp_in": _bf(w_mlp_in[l]), "w_mlp_out": _bf(w_mlp_out[l]),
        }
        x2 = _layer(x2, batch, seq, p)
    return x2.reshape(batch, seq, d)
```

```python
import functools
import math

import jax
import jax.numpy as jnp
import numpy as np
from jax import lax
from jax.experimental import pallas as pl
from jax.experimental.pallas import tpu as pltpu

F32 = jnp.float32
BF16 = jnp.bfloat16

D_MODEL = 1024
CHUNK = 64
EPS = 1e-6
HEADS = 4
HEAD_DIM = 64
BRANCH_WIDTH = HEADS * HEAD_DIM
D_FF = 4 * D_MODEL
CONV_K = 4
MLA_NOPE = HEAD_DIM
MLA_ROPE = HEAD_DIM // 2
MLA_Q_LORA = D_MODEL // 4
MLA_KV_LORA = D_MODEL // 8
ROPE_THETA = 10000.0
GLA_DK = HEAD_DIM // 2
GLA_RANK = 16
GLA_TAU = 16.0
LOG2E = math.log2(math.e)
LANE = 128
SLOT = LANE

A_WIDTH = 3 * BRANCH_WIDTH + 3 * BRANCH_WIDTH
B_WIDTH = MLA_Q_LORA + MLA_KV_LORA + 2 * SLOT
C_WIDTH = 2 * HEADS * GLA_DK + 2 * BRANCH_WIDTH + LANE
D_WIDTH = 3 * BRANCH_WIDTH + LANE
IN_TOTAL = A_WIDTH + B_WIDTH + C_WIDTH + D_WIDTH

VMEM_LIMIT = 56 * 1024 * 1024


def _dot(a, b):
    return jnp.dot(a, b, preferred_element_type=F32)


def _dot_nt(a, b):
    return lax.dot_general(a, b, (((1,), (1,)), ((), ())), preferred_element_type=F32)


def _dot_tn(a, b):
    return lax.dot_general(a, b, (((0,), (0,)), ((), ())), preferred_element_type=F32)


def _bf(x):
    return x.astype(BF16)


def _split(x, pieces):
    out = []
    for _ in range(pieces - 1):
        hi = _bf(x)
        out.append(hi)
        x = x - hi.astype(F32)
    out.append(_bf(x))
    return out


def _dot3(a, b):
    ah, al = _split(a, 2)
    bh, bl = _split(b, 2)
    return _dot(ah, bh) + (_dot(ah, bl) + _dot(al, bh))


def _dot_ones(ones_bf, x):
    x1, x2, x3 = _split(x, 3)
    return _dot(ones_bf, x1) + (_dot(ones_bf, x2) + _dot(ones_bf, x3))


def _rms(x, gain):
    return x * lax.rsqrt(jnp.mean(x * x, axis=-1, keepdims=True) + EPS) * gain


def _silu(x):
    return x * jax.nn.sigmoid(x)


def _softplus(x):
    return jnp.maximum(x, 0.0) + jnp.log1p(jnp.exp(-jnp.abs(x)))


def _log_sigmoid(x):
    return jnp.minimum(x, 0.0) - jnp.log1p(jnp.exp(-jnp.abs(x)))


def _tri(n, strict=False):
    r = lax.broadcasted_iota(jnp.int32, (n, n), 0)
    c = lax.broadcasted_iota(jnp.int32, (n, n), 1)
    return (c < r) if strict else (c <= r)


def _params(*sem):
    return pltpu.CompilerParams(dimension_semantics=sem, vmem_limit_bytes=VMEM_LIMIT)


def _const_spec(shape):
    nd = len(shape)
    return pl.BlockSpec(shape, lambda *_: (0,) * nd, pipeline_mode=pl.Buffered(1))


def _inproj_kernel(x_ref, gain_ref, w_ref, convw_ref, wgu_ref, gbias_ref,
                   oa_ref, ob_ref, oc_ref, od_ref, xe_ref, *, tm, tiles_per_seq):
    i = pl.program_id(0)
    h = _bf(_rms(x_ref[...], gain_ref[...]))

    qkv_w = 3 * BRANCH_WIDTH
    pa = _dot(h, w_ref[:, 0:A_WIDTH])
    raw = pa[:, 0:qkv_w]

    @pl.when(i % tiles_per_seq == 0)
    def _():
        xe_ref[0:8, :] = jnp.zeros((8, qkv_w), F32)

    xe_ref[8:8 + tm, :] = raw
    y = jnp.zeros((tm, qkv_w), F32)
    for j in range(CONV_K):
        y = y + convw_ref[j:j + 1, :] * xe_ref[pl.ds(8 - (CONV_K - 1) + j, tm), :]
    tail = xe_ref[tm:tm + 8, :]
    xe_ref[0:8, :] = tail
    oa_ref[:, 0:qkv_w] = _silu(y)
    oa_ref[:, qkv_w:A_WIDTH] = pa[:, qkv_w:A_WIDTH]

    off = A_WIDTH
    ob_ref[...] = _dot(h, w_ref[:, off:off + B_WIDTH])
    off += B_WIDTH
    pc = _dot(h, w_ref[:, off:off + C_WIDTH])
    lr_off = C_WIDTH - LANE
    oc_ref[:, 0:lr_off] = pc[:, 0:lr_off]
    gate = _dot(_bf(pc[:, lr_off:C_WIDTH]), wgu_ref[...]) + gbias_ref[...]
    oc_ref[:, lr_off:C_WIDTH] = _log_sigmoid(gate) * (1.0 / GLA_TAU)
    off += C_WIDTH
    od_ref[...] = _dot(h, w_ref[:, off:off + D_WIDTH])


def _inproj(x2, gain, w_cat, convw, wgu, gbias, seq, tm=256):
    t = x2.shape[0]
    grid = (t // tm,)
    row = lambda w: pl.BlockSpec((tm, w), lambda i: (i, 0))
    return pl.pallas_call(
        functools.partial(_inproj_kernel, tm=tm, tiles_per_seq=seq // tm),
        grid=grid,
        in_specs=[row(D_MODEL), _const_spec((1, D_MODEL)), _const_spec((D_MODEL, IN_TOTAL)),
                  _const_spec((CONV_K, 3 * BRANCH_WIDTH)), _const_spec((LANE, LANE)),
                  _const_spec((1, LANE))],
        out_specs=[row(A_WIDTH), row(B_WIDTH), row(C_WIDTH), row(D_WIDTH)],
        out_shape=[jax.ShapeDtypeStruct((t, w), F32) for w in (A_WIDTH, B_WIDTH, C_WIDTH, D_WIDTH)],
        scratch_shapes=[pltpu.VMEM((tm + 8, 3 * BRANCH_WIDTH), F32)],
        compiler_params=_params("arbitrary"),
        name="inproj",
    )(x2, gain, w_cat, convw, wgu, gbias)


def _head_ones():
    r = lax.broadcasted_iota(jnp.int32, (BRANCH_WIDTH, BRANCH_WIDTH), 0)
    c = lax.broadcasted_iota(jnp.int32, (BRANCH_WIDTH, BRANCH_WIDTH), 1)
    shift = HEAD_DIM.bit_length() - 1
    return _bf(jnp.where((r >> shift) == (c >> shift), 1.0, 0.0))


def _head_sums(x, ones_bd):
    x1, x2 = _split(x, 2)
    return _dot(x1, ones_bd) + _dot(x2, ones_bd)


def _head_rms(x, ones_bd, gain):
    return x * lax.rsqrt(_head_sums(x * x, ones_bd) * (1.0 / HEAD_DIM) + EPS) * gain


def _deltanet_kernel(a_ref, alog_ref, dtb_ref, norm_ref, o_ref, state_ref, *, tb, blocks_per_seq):
    i = pl.program_id(0)

    @pl.when(i % blocks_per_seq == 0)
    def _():
        state_ref[...] = jnp.zeros_like(state_ref)

    bw = BRANCH_WIDTH
    nchunk = tb // CHUNK
    row = lax.broadcasted_iota(jnp.int32, (CHUNK, CHUNK), 0)
    col = lax.broadcasted_iota(jnp.int32, (CHUNK, CHUNK), 1)
    lower = col <= row
    strict = col < row
    same_block = lambda log2s: (row >> log2s) == (col >> log2s)
    lmat = _bf(jnp.where(lower, 1.0, 0.0))
    smat = jnp.where(strict, 1.0, 0.0).astype(F32)
    eye = jnp.where(row == col, 1.0, 0.0).astype(F32)
    ones_bd = _head_ones()

    neg_a = -jnp.exp(alog_ref[...])
    g_all = neg_a * _softplus(a_ref[:, 4 * bw:5 * bw] + dtb_ref[...])
    beta_all = jax.nn.sigmoid(a_ref[:, 5 * bw:6 * bw])
    q_all = a_ref[:, 0:bw]
    k_all = a_ref[:, bw:2 * bw]
    q_all = q_all * lax.rsqrt(_head_sums(q_all * q_all, ones_bd) + EPS) * (HEAD_DIM ** -0.5)
    k_all = k_all * lax.rsqrt(_head_sums(k_all * k_all, ones_bd) + EPS)
    vb_all = a_ref[:, 2 * bw:3 * bw] * beta_all

    idx = [(c, h) for c in range(nchunk) for h in range(HEADS)]

    def sub(x, c, h):
        return x[c * CHUNK:(c + 1) * CHUNK, h * HEAD_DIM:(h + 1) * HEAD_DIM]

    qs = [sub(q_all, c, h) for c, h in idx]
    ks = [sub(k_all, c, h) for c, h in idx]
    gs = [sub(g_all, c, h) for c, h in idx]
    betas = [sub(beta_all, c, h) for c, h in idx]

    gcs = [_dot_ones(lmat, g) for g in gs]
    diffs = [_dot_ones(lmat, g * smat) for g in gs]
    decays = [jnp.where(lower, jnp.exp(jnp.where(lower, d, 0.0)), 0.0) for d in diffs]
    kbs = [_bf(k) for k in ks]
    kks = [_dot_nt(kb, kb) for kb in kbs]
    amats = [jnp.where(strict, b * kk * d, 0.0) for b, kk, d in zip(betas, kks, decays)]

    xs = [eye - jnp.where(same_block(1), a, 0.0) for a in amats]
    for log2s in range(1, CHUNK.bit_length() - 1):
        off = same_block(log2s + 1) & ~same_block(log2s)
        ns = [_bf(jnp.where(off, a, 0.0)) for a in amats]
        xbs = [_bf(x) for x in xs]
        xns = [_bf(_dot(xb, n)) for xb, n in zip(xbs, ns)]
        xs = [x - _dot(xn, xb) for x, xn, xb in zip(xs, xns, xbs)]
    resid = [eye - _dot3(eye + a, x) for a, x in zip(amats, xs)]
    xs = [x + _dot(_bf(x), _bf(r)) for x, r in zip(xs, resid)]

    egcs = [jnp.exp(gc) for gc in gcs]
    us = [_dot3(x, sub(vb_all, c, h)) for x, (c, h) in zip(xs, idx)]
    wbs = [_bf(_dot(_bf(x), _bf(k * b * e))) for x, k, b, e in zip(xs, ks, betas, egcs)]
    qks = [_bf(_dot_nt(_bf(q), kb) * d) for q, kb, d in zip(qs, kbs, decays)]
    q_decays = [_bf(q * e) for q, e in zip(qs, egcs)]
    lasts = [gc[CHUNK - 1:CHUNK, :] for gc in gcs]
    k_tails = [_bf(k * jnp.exp(l - gc)) for k, l, gc in zip(ks, lasts, gcs)]
    cds = [jnp.exp(l) for l in lasts]

    states = [state_ref[h] for h in range(HEADS)]
    o_rows = []
    for c in range(nchunk):
        outs = []
        for h in range(HEADS):
            n = c * HEADS + h
            s = states[h]
            sb = _bf(s)
            vb = _bf(us[n] - _dot(wbs[n], sb))
            outs.append(_dot(q_decays[n], sb) + _dot(qks[n], vb))
            states[h] = s * cds[n] + _dot_tn(k_tails[n], vb)
        o_rows.append(jnp.concatenate(outs, axis=1))
    for h in range(HEADS):
        state_ref[h] = states[h]
    o_all = jnp.concatenate(o_rows, axis=0)
    o_ref[...] = _bf(_head_rms(o_all, ones_bd, norm_ref[...]) * _silu(a_ref[:, 3 * bw:4 * bw]))


def _deltanet(ga, alog, dtb, norm, seq, tb=256):
    t = ga.shape[0]
    return pl.pallas_call(
        functools.partial(_deltanet_kernel, tb=tb, blocks_per_seq=seq // tb),
        grid=(t // tb,),
        in_specs=[pl.BlockSpec((tb, A_WIDTH), lambda i: (i, 0)),
                  _const_spec((1, BRANCH_WIDTH)), _const_spec((1, BRANCH_WIDTH)),
                  _const_spec((1, BRANCH_WIDTH))],
        out_specs=pl.BlockSpec((tb, BRANCH_WIDTH), lambda i: (i, 0)),
        out_shape=jax.ShapeDtypeStruct((t, BRANCH_WIDTH), BF16),
        scratch_shapes=[pltpu.VMEM((HEADS, HEAD_DIM, HEAD_DIM), F32)],
        compiler_params=_params("arbitrary"),
        name="deltanet",
    )(ga, alog, dtb, norm)


def _gla_kernel(c_ref, norm_ref, o_ref, state_ref, *, tb, blocks_per_seq):
    i = pl.program_id(0)

    @pl.when(i % blocks_per_seq == 0)
    def _():
        state_ref[...] = jnp.zeros_like(state_ref)

    qw = HEADS * GLA_DK
    bw = BRANCH_WIDTH
    nchunk = tb // CHUNK
    lower = _tri(CHUNK)
    lmat = _bf(jnp.where(lower, 1.0, 0.0))
    la_off = 2 * qw + 2 * bw
    idx = [(c, h) for c in range(nchunk) for h in range(HEADS)]

    def blk(col0, width, c, h):
        return c_ref[c * CHUNK:(c + 1) * CHUNK, col0 + h * width:col0 + (h + 1) * width]

    bcums = [_dot_ones(lmat, blk(la_off, GLA_DK, c, h)) for c, h in idx]
    q_ins = [_bf(blk(0, GLA_DK, c, h) * (GLA_DK ** -0.5) * jnp.exp(b)) for (c, h), b in zip(idx, bcums)]
    k_ins = [_bf(blk(qw, GLA_DK, c, h) * jnp.exp(-b)) for (c, h), b in zip(idx, bcums)]
    vbs = [_bf(blk(2 * qw, HEAD_DIM, c, h)) for c, h in idx]
    intras = [_dot(_bf(jnp.where(lower, _dot_nt(qi, ki), 0.0)), vb) for qi, ki, vb in zip(q_ins, k_ins, vbs)]
    lasts = [b[CHUNK - 1:CHUNK, :] for b in bcums]
    k_tails = [_bf(blk(qw, GLA_DK, c, h) * jnp.exp(l - b)) for (c, h), l, b in zip(idx, lasts, bcums)]
    cds = [jnp.exp(l) for l in lasts]

    states = [state_ref[h] for h in range(HEADS)]
    o_rows = []
    for c in range(nchunk):
        outs = []
        for h in range(HEADS):
            n = c * HEADS + h
            outs.append(intras[n] + _dot_nt(q_ins[n], _bf(states[h])))
            states[h] = states[h] * cds[n] + _dot_tn(vbs[n], k_tails[n])
        o_rows.append(jnp.concatenate(outs, axis=1))
    for h in range(HEADS):
        state_ref[h] = states[h]
    o_all = jnp.concatenate(o_rows, axis=0)
    gate = c_ref[:, 2 * qw + bw:2 * qw + 2 * bw]
    o_ref[...] = _bf(_head_rms(o_all, _head_ones(), norm_ref[...]) * _silu(gate))


def _gla(gc, norm, seq, tb=256):
    t = gc.shape[0]
    return pl.pallas_call(
        functools.partial(_gla_kernel, tb=tb, blocks_per_seq=seq // tb),
        grid=(t // tb,),
        in_specs=[pl.BlockSpec((tb, C_WIDTH), lambda i: (i, 0)), _const_spec((1, BRANCH_WIDTH))],
        out_specs=pl.BlockSpec((tb, BRANCH_WIDTH), lambda i: (i, 0)),
        out_shape=jax.ShapeDtypeStruct((t, BRANCH_WIDTH), BF16),
        scratch_shapes=[pltpu.VMEM((HEADS, HEAD_DIM, GLA_DK), F32)],
        compiler_params=_params("arbitrary"),
        name="gla",
    )(gc, norm)


def _mla_prep_kernel(b_ref, qn_ref, kvn_ref, wq_ref, wkv_ref, cos_ref, sin_ref,
                     q_ref, k_ref, v_ref):
    cq = b_ref[:, 0:MLA_Q_LORA]
    ckv = b_ref[:, MLA_Q_LORA:MLA_Q_LORA + MLA_KV_LORA]
    off = MLA_Q_LORA + MLA_KV_LORA
    kpe = b_ref[:, off:off + SLOT]
    kpe_rot = b_ref[:, off + SLOT:off + 2 * SLOT]
    cos = cos_ref[...]
    sin = sin_ref[...]

    qa = _dot(_bf(_rms(cq, qn_ref[...])), wq_ref[...])
    kv = _dot(_bf(_rms(ckv, kvn_ref[...])), wkv_ref[...])
    k_pe = kpe * cos + kpe_rot * sin
    nslot = HEADS * SLOT
    for h in range(HEADS):
        sl = slice(h * SLOT, (h + 1) * SLOT)
        q_rope = qa[:, sl] * cos + qa[:, nslot + h * SLOT:nslot + (h + 1) * SLOT] * sin
        q_ref[:, sl] = _bf(q_rope * ((MLA_NOPE + MLA_ROPE) ** -0.5 * LOG2E))
        k_ref[:, sl] = _bf(kv[:, sl] + k_pe)
    v_ref[...] = _bf(kv[:, nslot:nslot + BRANCH_WIDTH])


def _mla_prep(gb, qn, kvn, wq, wkv, cos_t, sin_t, seq, tm=512):
    t = gb.shape[0]
    tps = seq // tm
    row = lambda w: pl.BlockSpec((tm, w), lambda i: (i, 0))
    pos = pl.BlockSpec((tm, SLOT), lambda i: (i % tps, 0))
    nslot = HEADS * SLOT
    return pl.pallas_call(
        _mla_prep_kernel,
        grid=(t // tm,),
        in_specs=[row(B_WIDTH), _const_spec((1, MLA_Q_LORA)), _const_spec((1, MLA_KV_LORA)),
                  _const_spec((MLA_Q_LORA, 2 * nslot)), _const_spec((MLA_KV_LORA, nslot + BRANCH_WIDTH)),
                  pos, pos],
        out_specs=[row(nslot), row(nslot), row(BRANCH_WIDTH)],
        out_shape=[jax.ShapeDtypeStruct((t, nslot), BF16), jax.ShapeDtypeStruct((t, nslot), BF16),
                   jax.ShapeDtypeStruct((t, BRANCH_WIDTH), BF16)],
        compiler_params=_params("parallel"),
        name="mla_prep",
    )(gb, qn, kvn, wq, wkv, cos_t, sin_t)


def _fox_prep_kernel(d_ref, fb_ref, q_ref, k_ref, v_ref, carry_ref, *, tb, blocks_per_seq):
    i = pl.program_id(0)

    @pl.when(i % blocks_per_seq == 0)
    def _():
        carry_ref[...] = jnp.zeros_like(carry_ref)

    bw = BRANCH_WIDTH
    log_f = _log_sigmoid(d_ref[:, 3 * bw:3 * bw + LANE] + fb_ref[...])
    lmat = _bf(jnp.where(_tri(tb), 1.0, 0.0))
    fcum = _dot_ones(lmat, log_f) + carry_ref[...]
    carry_ref[...] = fcum[tb - 1:tb, :]
    f1, f2, f3 = [f.astype(F32) for f in _split(fcum * LOG2E, 3)]

    lane = lax.broadcasted_iota(jnp.int32, (tb, LANE), 1)
    scale = HEAD_DIM ** -0.5 * LOG2E
    for h in range(HEADS):
        pair, odd = divmod(h, 2)
        qp = d_ref[:, pair * LANE:(pair + 1) * LANE] * scale
        kp = d_ref[:, bw + pair * LANE:bw + (pair + 1) * LANE]
        head_lanes = (lane >= HEAD_DIM) if odd else (lane < HEAD_DIM)
        a0 = 0 if odd else HEAD_DIM
        fb = [jnp.broadcast_to(f[:, h:h + 1], (tb, LANE)) for f in (f1, f2, f3)]
        qx = jnp.zeros((tb, LANE), F32)
        kx = jnp.zeros((tb, LANE), F32)
        for j in range(3):
            qx = jnp.where(lane == a0 + j, fb[j], qx)
            qx = jnp.where(lane == a0 + 3 + j, 1.0, qx)
            kx = jnp.where(lane == a0 + j, 1.0, kx)
            kx = jnp.where(lane == a0 + 3 + j, -fb[j], kx)
        q_ref[:, h * SLOT:(h + 1) * SLOT] = _bf(jnp.where(head_lanes, qp, qx))
        k_ref[:, h * SLOT:(h + 1) * SLOT] = _bf(jnp.where(head_lanes, kp, kx))
    v_ref[...] = _bf(d_ref[:, 2 * bw:3 * bw])


def _fox_prep(gd, fbias, seq, tb=256):
    t = gd.shape[0]
    nslot = HEADS * SLOT
    row = lambda w: pl.BlockSpec((tb, w), lambda i: (i, 0))
    return pl.pallas_call(
        functools.partial(_fox_prep_kernel, tb=tb, blocks_per_seq=seq // tb),
        grid=(t // tb,),
        in_specs=[row(D_WIDTH), _const_spec((1, LANE))],
        out_specs=[row(nslot), row(nslot), row(BRANCH_WIDTH)],
        out_shape=[jax.ShapeDtypeStruct((t, nslot), BF16), jax.ShapeDtypeStruct((t, nslot), BF16),
                   jax.ShapeDtypeStruct((t, BRANCH_WIDTH), BF16)],
        scratch_shapes=[pltpu.VMEM((1, LANE), F32)],
        compiler_params=_params("arbitrary"),
        name="fox_prep",
    )(gd, fbias)


VT_ROWS = HEAD_DIM + 16


def _flash_kernel(q_ref, k_ref, vt_ref, o_ref, m_ref, acc_ref, s_ref, *, chunk_mask, tq, tk):
    assert tq == tk
    qi = pl.program_id(1)
    m_ref[...] = jnp.full_like(m_ref, -jnp.inf)
    acc_ref[...] = jnp.zeros_like(acc_ref)

    def scores(ki, slot, hh):
        s_ref[slot, hh] = _dot_nt(k_ref[ki, :, hh * SLOT:(hh + 1) * SLOT],
                                  q_ref[:, hh * SLOT:(hh + 1) * SLOT])

    def softmax_pv(ki, slot, hh, masked):
        s = s_ref[slot, hh]
        if masked:
            kpos = lax.broadcasted_iota(jnp.int32, s.shape, 0)
            qpos = lax.broadcasted_iota(jnp.int32, s.shape, 1)
            if chunk_mask:
                shift = CHUNK.bit_length() - 1
                ok = (kpos >> shift) <= (qpos >> shift)
            else:
                ok = kpos <= qpos
            s = jnp.where(ok, s, -jnp.inf)
        m_prev = m_ref[hh]
        m_new = jnp.maximum(m_prev, jnp.max(s, axis=0, keepdims=True))
        p = _bf(jnp.exp2(s - m_new))
        pv = _dot(vt_ref[ki, hh * VT_ROWS:(hh + 1) * VT_ROWS, :], p)
        acc_ref[hh] = jnp.exp2(m_prev - m_new) * acc_ref[hh] + pv
        m_ref[hh] = m_new

    def step(ki, slot, masked=False, prefetch=True):
        for hh in range(HEADS):
            if prefetch:
                scores(ki + 1, 1 - slot, hh)
            softmax_pv(ki, slot, hh, masked)

    for hh in range(HEADS):
        scores(0, 0, hh)

    def pair(j, carry):
        step(2 * j, 0)
        step(2 * j + 1, 1)
        return carry

    lax.fori_loop(0, qi // 2, pair, 0)

    @pl.when(qi % 2 == 0)
    def _():
        step(qi, 0, masked=True, prefetch=False)

    @pl.when(qi % 2 == 1)
    def _():
        step(qi - 1, 0)
        step(qi, 1, masked=True, prefetch=False)

    ot = jnp.concatenate([acc_ref[hh][0:HEAD_DIM] / acc_ref[hh][HEAD_DIM:HEAD_DIM + 1]
                          for hh in range(HEADS)], axis=0)
    o_ref[...] = _bf(ot.T)


def _flash(q, k, v, batch, seq, chunk_mask, tq=512, tk=512):
    nq = seq // tq
    nk = seq // tk
    t = q.shape[0]
    k3 = k.reshape(t // tk, tk, HEADS * SLOT)
    vt = v.reshape(t // tk, tk, HEADS, HEAD_DIM).transpose(0, 2, 3, 1)
    ones = jnp.ones((t // tk, HEADS, 1, tk), BF16)
    zeros = jnp.zeros((t // tk, HEADS, VT_ROWS - HEAD_DIM - 1, tk), BF16)
    vt3 = jnp.concatenate([vt, ones, zeros], axis=2).reshape(t // tk, HEADS * VT_ROWS, tk)
    return pl.pallas_call(
        functools.partial(_flash_kernel, chunk_mask=chunk_mask, tq=tq, tk=tk),
        grid=(batch, nq),
        in_specs=[pl.BlockSpec((tq, HEADS * SLOT), lambda b, qi: (b * nq + qi, 0)),
                  pl.BlockSpec((nk, tk, HEADS * SLOT), lambda b, qi: (b, 0, 0)),
                  pl.BlockSpec((nk, HEADS * VT_ROWS, tk), lambda b, qi: (b, 0, 0))],
        out_specs=pl.BlockSpec((tq, BRANCH_WIDTH), lambda b, qi: (b * nq + qi, 0)),
        out_shape=jax.ShapeDtypeStruct((t, BRANCH_WIDTH), BF16),
        scratch_shapes=[pltpu.VMEM((HEADS, 1, tq), F32), pltpu.VMEM((HEADS, VT_ROWS, tq), F32),
                        pltpu.VMEM((2, HEADS, tk, tq), F32)],
        compiler_params=_params("parallel", "arbitrary"),
        name="flash_chunk" if chunk_mask else "flash_causal",
    )(q, k3, vt3)


def _merge_kernel(x_ref, oa_ref, ob_ref, oc_ref, od_ref, gpre_ref, wg_ref, bg_ref, wup_ref,
                  wout_ref, gpost_ref, o_ref):
    x = x_ref[...]
    h = _bf(_rms(x, gpre_ref[...]))
    merged = jnp.zeros(x.shape, F32)
    for i, br in enumerate((oa_ref, ob_ref, oc_ref, od_ref)):
        gate = jax.nn.sigmoid(_dot(h, wg_ref[i]) + bg_ref[i:i + 1, :])
        merged = merged + gate * _dot(br[...], wup_ref[i])
    y = _dot(_bf(merged), wout_ref[...])
    o_ref[...] = x + _rms(y, gpost_ref[...])


def _merge(x2, oa, ob, oc, od, gpre, wg, bg, wup, wout, gpost, tm=512):
    t = x2.shape[0]
    row = lambda w: pl.BlockSpec((tm, w), lambda i: (i, 0))
    return pl.pallas_call(
        _merge_kernel,
        grid=(t // tm,),
        in_specs=[row(D_MODEL)] + [row(BRANCH_WIDTH)] * 4 +
                 [_const_spec((1, D_MODEL)), _const_spec((4, D_MODEL, D_MODEL)),
                  _const_spec((4, D_MODEL)), _const_spec((4, BRANCH_WIDTH, D_MODEL)),
                  _const_spec((D_MODEL, D_MODEL)), _const_spec((1, D_MODEL))],
        out_specs=row(D_MODEL),
        out_shape=jax.ShapeDtypeStruct((t, D_MODEL), F32),
        compiler_params=_params("parallel"),
        name="merge",
    )(x2, oa, ob, oc, od, gpre, wg, bg, wup, wout, gpost)


def _mlp_kernel(x_ref, gpre_ref, w1_ref, w2_ref, gpost_ref, o_ref, *, ff_chunk):
    x = x_ref[...]
    h = _bf(_rms(x, gpre_ref[...]))
    y = jnp.zeros(x.shape, F32)
    for c in range(D_FF // ff_chunk):
        u = jnp.maximum(_dot(h, w1_ref[:, c * ff_chunk:(c + 1) * ff_chunk]), 0.0)
        y = y + _dot(_bf(u * u), w2_ref[c * ff_chunk:(c + 1) * ff_chunk, :])
    o_ref[...] = x + _rms(y, gpost_ref[...])


def _mlp(x2, gpre, w1, w2, gpost, tm=512, ff_chunk=1024):
    t = x2.shape[0]
    row = pl.BlockSpec((tm, D_MODEL), lambda i: (i, 0))
    return pl.pallas_call(
        functools.partial(_mlp_kernel, ff_chunk=ff_chunk),
        grid=(t // tm,),
        in_specs=[row, _const_spec((1, D_MODEL)), _const_spec((D_MODEL, D_FF)),
                  _const_spec((D_FF, D_MODEL)), _const_spec((1, D_MODEL))],
        out_specs=row,
        out_shape=jax.ShapeDtypeStruct((t, D_MODEL), F32),
        compiler_params=_params("parallel"),
        name="mlp",
    )(x2, gpre, w1, w2, gpost)


_IN_SIZES = (
    HEADS * 3 * HEAD_DIM, HEADS * HEAD_DIM, HEADS, HEADS,
    MLA_Q_LORA, MLA_KV_LORA, MLA_ROPE,
    HEADS * GLA_DK, HEADS * GLA_DK, HEADS * HEAD_DIM, HEADS * HEAD_DIM, GLA_RANK,
    HEADS * HEAD_DIM, HEADS * HEAD_DIM, HEADS * HEAD_DIM, HEADS,
)


def _rot_cols(w):
    half = w.shape[-1] // 2
    return jnp.concatenate([-w[..., half:], w[..., :half]], axis=-1)


def _pad_cols(w, width):
    return jnp.pad(w, ((0, 0), (0, width - w.shape[-1])))


def _layout_in_weight(w_in):
    splits = np.cumsum(_IN_SIZES)[:-1].tolist()
    (dn_qkv, dn_z, dn_b, dn_a, cq, ckv, kpe, gq, gk, gv, gg, glr, fq, fk, fv, ff) = jnp.split(w_in, splits, axis=1)
    zeros = lambda n: jnp.zeros((w_in.shape[0], n), w_in.dtype)
    pe_slot = lambda w: jnp.concatenate([zeros(MLA_NOPE), w, zeros(SLOT - MLA_NOPE - MLA_ROPE)], axis=1)
    cols = [dn_qkv, dn_z, jnp.repeat(dn_a, HEAD_DIM, axis=1), jnp.repeat(dn_b, HEAD_DIM, axis=1),
            cq, ckv, pe_slot(kpe), pe_slot(_rot_cols(kpe)),
            gq, gk, gv, gg, _pad_cols(glr, LANE),
            fq, fk, fv, _pad_cols(ff, LANE)]
    w = jnp.concatenate(cols, axis=1)
    assert w.shape[1] == IN_TOTAL
    return _bf(w)


def _layout_q_up(w):
    zeros = lambda n: jnp.zeros((w.shape[0], n), w.dtype)
    per = MLA_NOPE + MLA_ROPE
    plain, rot = [], []
    for h in range(HEADS):
        nope = w[:, h * per:h * per + MLA_NOPE]
        pe = w[:, h * per + MLA_NOPE:(h + 1) * per]
        plain += [nope, pe, zeros(SLOT - per)]
        rot += [zeros(MLA_NOPE), _rot_cols(pe), zeros(SLOT - per)]
    return _bf(jnp.concatenate(plain + rot, axis=1))


def _layout_kv_up(w):
    zeros = lambda n: jnp.zeros((w.shape[0], n), w.dtype)
    per = MLA_NOPE + HEAD_DIM
    ks, vs = [], []
    for h in range(HEADS):
        ks += [w[:, h * per:h * per + MLA_NOPE], zeros(SLOT - MLA_NOPE)]
        vs += [w[:, h * per + MLA_NOPE:(h + 1) * per]]
    return _bf(jnp.concatenate(ks + vs, axis=1))


def _rope_tables(seq):
    inv_freq = 1.0 / (ROPE_THETA ** (jnp.arange(0, MLA_ROPE, 2, dtype=F32) / MLA_ROPE))
    ang = jnp.arange(seq, dtype=F32)[:, None] * inv_freq[None, :]
    ang = jnp.concatenate([ang, ang], axis=-1)
    pad = jnp.zeros((seq, SLOT - MLA_NOPE - MLA_ROPE), F32)
    cos_t = jnp.concatenate([jnp.ones((seq, MLA_NOPE), F32), jnp.cos(ang), pad], axis=1)
    sin_t = jnp.concatenate([jnp.zeros((seq, MLA_NOPE), F32), jnp.sin(ang), pad], axis=1)
    return cos_t, sin_t


def _layer(x2, batch, seq, p):
    ga, gb, gc, gd = _inproj(x2, p["norm_mix_pre"], p["w_cat"], p["dn_conv"], p["wgu"], p["gbias"], seq)
    o_a = _deltanet(ga, p["alog"], p["dtb"], p["dn_norm"], seq)
    q_b, k_b, v_b = _mla_prep(gb, p["mla_q_norm"], p["mla_kv_norm"], p["wq"], p["wkv"],
                              p["cos"], p["sin"], seq)
    o_b = _flash(q_b, k_b, v_b, batch, seq, True)
    o_c = _gla(gc, p["gla_norm"], seq)
    q_d, k_d, v_d = _fox_prep(gd, p["fbias"], seq)
    o_d = _flash(q_d, k_d, v_d, batch, seq, False)
    x2 = _merge(x2, o_a, o_b, o_c, o_d, p["norm_mix_pre"], p["w_gate"], p["b_gate"],
                p["w_branch_up"], p["w_out"], p["norm_mix_post"])
    return _mlp(x2, p["norm_mlp_pre"], p["w_mlp_in"], p["w_mlp_out"], p["norm_mlp_post"])


def kernel(x, norm_mix_pre, norm_mix_post, norm_mlp_pre, norm_mlp_post, w_in, dn_conv, dn_a_log, dn_dt_bias, dn_norm, mla_q_norm, mla_w_q_up, mla_kv_norm, mla_w_kv_up, gla_w_gate_up, gla_gate_bias, gla_norm, fox_f_bias, w_branch_up, w_gate, b_gate, w_out, w_mlp_in, w_mlp_out):
    batch, seq, d = x.shape
    depth = w_in.shape[0]
    cos_t, sin_t = _rope_tables(seq)
    x2 = x.reshape(batch * seq, d)
    row = lambda a: a.reshape(1, -1)
    for l in range(depth):
        p = {
            "norm_mix_pre": row(norm_mix_pre[l]), "norm_mix_post": row(norm_mix_post[l]),
            "norm_mlp_pre": row(norm_mlp_pre[l]), "norm_mlp_post": row(norm_mlp_post[l]),
            "w_cat": _layout_in_weight(w_in[l]),
            "dn_conv": dn_conv[l],
            "wgu": _bf(jnp.pad(gla_w_gate_up[l], ((0, LANE - GLA_RANK), (0, 0)))),
            "gbias": row(gla_gate_bias[l]),
            "alog": row(jnp.repeat(dn_a_log[l], HEAD_DIM)), "dtb": row(jnp.repeat(dn_dt_bias[l], HEAD_DIM)),
            "dn_norm": row(jnp.tile(dn_norm[l], HEADS)),
            "mla_q_norm": row(mla_q_norm[l]), "mla_kv_norm": row(mla_kv_norm[l]),
            "wq": _layout_q_up(mla_w_q_up[l]), "wkv": _layout_kv_up(mla_w_kv_up[l]),
            "cos": cos_t, "sin": sin_t,
            "gla_norm": row(jnp.tile(gla_norm[l], HEADS)),
            "fbias": row(jnp.pad(fox_f_bias[l], (0, LANE - HEADS))),
            "w_gate": _bf(w_gate[l]), "b_gate": b_gate[l], "w_branch_up": _bf(w_branch_up[l]),
            "w_out": _bf(w_out[l]), "w_mlp_in": _bf(w_mlp_in[l]), "w_mlp_out": _bf(w_mlp_out[l]),
        }
        x2 = _layer(x2, batch, seq, p)
    return x2.reshape(batch, seq, d)
```

```python
import functools
import math

import jax
import jax.numpy as jnp
import numpy as np
from jax import lax
from jax.experimental import pallas as pl
from jax.experimental.pallas import tpu as pltpu

F32 = jnp.float32
BF16 = jnp.bfloat16

D_MODEL = 1024
CHUNK = 64
EPS = 1e-6
HEADS = 4
HEAD_DIM = 64
BRANCH_WIDTH = HEADS * HEAD_DIM
D_FF = 4 * D_MODEL
CONV_K = 4
MLA_NOPE = HEAD_DIM
MLA_ROPE = HEAD_DIM // 2
MLA_Q_LORA = D_MODEL // 4
MLA_KV_LORA = D_MODEL // 8
ROPE_THETA = 10000.0
GLA_DK = HEAD_DIM // 2
GLA_RANK = 16
GLA_TAU = 16.0
LOG2E = math.log2(math.e)
LANE = 128
SLOT = LANE

A_WIDTH = 3 * BRANCH_WIDTH + 3 * BRANCH_WIDTH
B_WIDTH = MLA_Q_LORA + MLA_KV_LORA + 2 * SLOT
C_WIDTH = 2 * HEADS * GLA_DK + 2 * BRANCH_WIDTH + LANE
D_WIDTH = 3 * BRANCH_WIDTH + LANE
IN_TOTAL = A_WIDTH + B_WIDTH + C_WIDTH + D_WIDTH

VMEM_LIMIT = 56 * 1024 * 1024


def _dot(a, b):
    return jnp.dot(a, b, preferred_element_type=F32)


def _dot_nt(a, b):
    return lax.dot_general(a, b, (((1,), (1,)), ((), ())), preferred_element_type=F32)


def _dot_tn(a, b):
    return lax.dot_general(a, b, (((0,), (0,)), ((), ())), preferred_element_type=F32)


def _bf(x):
    return x.astype(BF16)


def _split(x, pieces):
    out = []
    for _ in range(pieces - 1):
        hi = _bf(x)
        out.append(hi)
        x = x - hi.astype(F32)
    out.append(_bf(x))
    return out


def _dot3(a, b):
    ah, al = _split(a, 2)
    bh, bl = _split(b, 2)
    return _dot(ah, bh) + (_dot(ah, bl) + _dot(al, bh))


def _dot_ones(ones_bf, x):
    x1, x2, x3 = _split(x, 3)
    return _dot(ones_bf, x1) + (_dot(ones_bf, x2) + _dot(ones_bf, x3))


def _rms(x, gain):
    return x * lax.rsqrt(jnp.mean(x * x, axis=-1, keepdims=True) + EPS) * gain


def _silu(x):
    return x * jax.nn.sigmoid(x)


def _softplus(x):
    return jnp.maximum(x, 0.0) + jnp.log1p(jnp.exp(-jnp.abs(x)))


def _log_sigmoid(x):
    return jnp.minimum(x, 0.0) - jnp.log1p(jnp.exp(-jnp.abs(x)))


def _tri(n, strict=False):
    r = lax.broadcasted_iota(jnp.int32, (n, n), 0)
    c = lax.broadcasted_iota(jnp.int32, (n, n), 1)
    return (c < r) if strict else (c <= r)


def _params(*sem):
    return pltpu.CompilerParams(dimension_semantics=sem, vmem_limit_bytes=VMEM_LIMIT)


def _const_spec(shape):
    nd = len(shape)
    return pl.BlockSpec(shape, lambda *_: (0,) * nd, pipeline_mode=pl.Buffered(1))


def _inproj_kernel(x_ref, gain_ref, w_ref, convw_ref, wgu_ref, gbias_ref,
                   oa_ref, ob_ref, oc_ref, od_ref, xe_ref, *, tm, tiles_per_seq):
    i = pl.program_id(0)
    h = _bf(_rms(x_ref[...], gain_ref[...]))

    qkv_w = 3 * BRANCH_WIDTH
    cw = BRANCH_WIDTH

    @pl.when(i % tiles_per_seq == 0)
    def _():
        xe_ref[0:8, :] = jnp.zeros((8, qkv_w), F32)

    def conv_cols(c0):
        y = jnp.zeros((tm, cw), F32)
        for j in range(CONV_K):
            y = y + convw_ref[j:j + 1, c0:c0 + cw] * xe_ref[pl.ds(8 - (CONV_K - 1) + j, tm), c0:c0 + cw]
        oa_ref[:, c0:c0 + cw] = _silu(y)
        xe_ref[0:8, c0:c0 + cw] = xe_ref[tm:tm + 8, c0:c0 + cw]

    for c0 in range(0, qkv_w, cw):
        xe_ref[8:8 + tm, c0:c0 + cw] = _dot(h, w_ref[:, c0:c0 + cw])
        if c0:
            conv_cols(c0 - cw)
    oa_ref[:, qkv_w:A_WIDTH] = _dot(h, w_ref[:, qkv_w:A_WIDTH])
    conv_cols(qkv_w - cw)

    off = A_WIDTH + B_WIDTH
    lr_off = C_WIDTH - LANE
    lr = _dot(h, w_ref[:, off + lr_off:off + C_WIDTH])
    ob_ref[...] = _dot(h, w_ref[:, A_WIDTH:A_WIDTH + B_WIDTH])
    gate = _dot(_bf(lr), wgu_ref[...]) + gbias_ref[...]
    oc_ref[:, lr_off:C_WIDTH] = _log_sigmoid(gate) * (1.0 / GLA_TAU)
    oc_ref[:, 0:lr_off] = _dot(h, w_ref[:, off:off + lr_off])
    off += C_WIDTH
    od_ref[...] = _dot(h, w_ref[:, off:off + D_WIDTH])


def _inproj(x2, gain, w_cat, convw, wgu, gbias, seq, tm=512):
    t = x2.shape[0]
    grid = (t // tm,)
    row = lambda w: pl.BlockSpec((tm, w), lambda i: (i, 0))
    return pl.pallas_call(
        functools.partial(_inproj_kernel, tm=tm, tiles_per_seq=seq // tm),
        grid=grid,
        in_specs=[row(D_MODEL), _const_spec((1, D_MODEL)), _const_spec((D_MODEL, IN_TOTAL)),
                  _const_spec((CONV_K, 3 * BRANCH_WIDTH)), _const_spec((LANE, LANE)),
                  _const_spec((1, LANE))],
        out_specs=[row(A_WIDTH), row(B_WIDTH), row(C_WIDTH), row(D_WIDTH)],
        out_shape=[jax.ShapeDtypeStruct((t, w), F32) for w in (A_WIDTH, B_WIDTH, C_WIDTH, D_WIDTH)],
        scratch_shapes=[pltpu.VMEM((tm + 8, 3 * BRANCH_WIDTH), F32)],
        compiler_params=_params("arbitrary"),
        name="inproj",
    )(x2, gain, w_cat, convw, wgu, gbias)


def _head_ones():
    r = lax.broadcasted_iota(jnp.int32, (BRANCH_WIDTH, BRANCH_WIDTH), 0)
    c = lax.broadcasted_iota(jnp.int32, (BRANCH_WIDTH, BRANCH_WIDTH), 1)
    shift = HEAD_DIM.bit_length() - 1
    return _bf(jnp.where((r >> shift) == (c >> shift), 1.0, 0.0))


def _head_sums(x, ones_bd):
    x1, x2 = _split(x, 2)
    return _dot(x1, ones_bd) + _dot(x2, ones_bd)


def _head_rms(x, ones_bd, gain):
    return x * lax.rsqrt(_head_sums(x * x, ones_bd) * (1.0 / HEAD_DIM) + EPS) * gain


def _deltanet_kernel(a_ref, alog_ref, dtb_ref, norm_ref, o_ref, state_ref, *, tb, blocks_per_seq):
    i = pl.program_id(0)

    @pl.when(i % blocks_per_seq == 0)
    def _():
        state_ref[...] = jnp.zeros_like(state_ref)

    bw = BRANCH_WIDTH
    nchunk = tb // CHUNK
    hshift = HEAD_DIM.bit_length() - 1
    t_idx = lax.broadcasted_iota(jnp.int32, (CHUNK, bw), 0)
    lane = lax.broadcasted_iota(jnp.int32, (CHUNK, bw), 1)
    s_idx = lane & (HEAD_DIM - 1)
    lower = s_idx <= t_idx
    strict = s_idx < t_idx
    same_block = lambda log2s: (t_idx >> log2s) == (s_idx >> log2s)
    eye = jnp.where(s_idx == t_idx, 1.0, 0.0).astype(F32)
    strict_f = jnp.where(strict, 1.0, 0.0).astype(F32)
    head_sel = [_bf(jnp.where((lane >> hshift) == h, 1.0, 0.0)) for h in range(HEADS)]
    r64 = lax.broadcasted_iota(jnp.int32, (CHUNK, CHUNK), 0)
    c64 = lax.broadcasted_iota(jnp.int32, (CHUNK, CHUNK), 1)
    lmat = _bf(jnp.where(c64 <= r64, 1.0, 0.0))
    r256 = lax.broadcasted_iota(jnp.int32, (bw, bw), 0)
    c256 = lax.broadcasted_iota(jnp.int32, (bw, bw), 1)
    same_head = (r256 >> hshift) == (c256 >> hshift)
    ones_bd = _bf(jnp.where(same_head, 1.0, 0.0))

    def bd(y):
        return jnp.concatenate([y * m for m in head_sel], axis=0)

    def mm(a, b):
        return _dot(a, bd(b))

    def mm3(a, b):
        ah, al = _split(a, 2)
        bh, bl = _split(b, 2)
        bdh = bd(bh)
        return _dot(ah, bdh) + (_dot(ah, bd(bl)) + _dot(al, bdh))

    neg_a = -jnp.exp(alog_ref[...])
    g_all = neg_a * _softplus(a_ref[:, 4 * bw:5 * bw] + dtb_ref[...])
    beta_all = jax.nn.sigmoid(a_ref[:, 5 * bw:6 * bw])
    q_all = a_ref[:, 0:bw]
    k_all = a_ref[:, bw:2 * bw]
    q_all = q_all * lax.rsqrt(_head_sums(q_all * q_all, ones_bd) + EPS) * (HEAD_DIM ** -0.5)
    k_all = k_all * lax.rsqrt(_head_sums(k_all * k_all, ones_bd) + EPS)
    vb_all = a_ref[:, 2 * bw:3 * bw] * beta_all

    chunks = range(nchunk)
    rows = lambda x, c: x[c * CHUNK:(c + 1) * CHUNK, :]
    qs = [rows(q_all, c) for c in chunks]
    ks = [rows(k_all, c) for c in chunks]
    gs = [rows(g_all, c) for c in chunks]
    betas = [rows(beta_all, c) for c in chunks]

    cums = []
    for g in gs:
        g1, g2 = _split(jnp.concatenate([g * strict_f, g], axis=1), 2)
        cums.append(_dot(lmat, g1) + _dot(lmat, g2))
    diffs = [cm[:, 0:bw] for cm in cums]
    gcs = [cm[:, bw:2 * bw] for cm in cums]
    decays = [jnp.where(lower, jnp.exp(jnp.where(lower, d, 0.0)), 0.0) for d in diffs]
    kbs = [_bf(k) for k in ks]
    kqs = [_dot_nt(jnp.concatenate([kb, _bf(q)], axis=0), bd(kb)) for kb, q in zip(kbs, qs)]
    amats = [jnp.where(strict, b * kq[0:CHUNK] * d, 0.0) for b, kq, d in zip(betas, kqs, decays)]

    xs = [eye - jnp.where(same_block(1), a, 0.0) for a in amats]
    for log2s in range(1, CHUNK.bit_length() - 1):
        off = same_block(log2s + 1) & ~same_block(log2s)
        ns = [_bf(jnp.where(off, a, 0.0)) for a in amats]
        xbs = [_bf(x) for x in xs]
        xns = [_bf(mm(xb, n)) for xb, n in zip(xbs, ns)]
        xs = [x - mm(xn, xb) for x, xn, xb in zip(xs, xns, xbs)]
    resid = [eye - mm3(eye + a, x) for a, x in zip(amats, xs)]
    xs = [x + mm(_bf(x), _bf(r)) for x, r in zip(xs, resid)]

    egcs = [jnp.exp(gc) for gc in gcs]
    us = [mm3(x, rows(vb_all, c)) for x, c in zip(xs, chunks)]
    wbs = [_bf(mm(_bf(x), _bf(k * b * e))) for x, k, b, e in zip(xs, ks, betas, egcs)]
    qks = [_bf(kq[CHUNK:2 * CHUNK] * d) for kq, d in zip(kqs, decays)]
    q_decays = [_bf(q * e) for q, e in zip(qs, egcs)]
    lasts = [gc[CHUNK - 1:CHUNK, :] for gc in gcs]
    k_tails = [_bf(k * jnp.exp(l - gc)) for k, l, gc in zip(ks, lasts, gcs)]
    cds = [jnp.exp(l) for l in lasts]

    s = state_ref[...]
    o_rows = []
    for c in chunks:
        sb = _bf(s)
        vb = _bf(us[c] - _dot(wbs[c], sb))
        o_rows.append(_dot(q_decays[c], sb) + mm(qks[c], vb))
        s = s * cds[c] + jnp.where(same_head, _dot_tn(k_tails[c], vb), 0.0)
    state_ref[...] = s
    o_all = jnp.concatenate(o_rows, axis=0)
    o_ref[...] = _bf(_head_rms(o_all, ones_bd, norm_ref[...]) * _silu(a_ref[:, 3 * bw:4 * bw]))


def _deltanet(ga, alog, dtb, norm, seq, tb=512):
    t = ga.shape[0]
    return pl.pallas_call(
        functools.partial(_deltanet_kernel, tb=tb, blocks_per_seq=seq // tb),
        grid=(t // tb,),
        in_specs=[pl.BlockSpec((tb, A_WIDTH), lambda i: (i, 0)),
                  _const_spec((1, BRANCH_WIDTH)), _const_spec((1, BRANCH_WIDTH)),
                  _const_spec((1, BRANCH_WIDTH))],
        out_specs=pl.BlockSpec((tb, BRANCH_WIDTH), lambda i: (i, 0)),
        out_shape=jax.ShapeDtypeStruct((t, BRANCH_WIDTH), BF16),
        scratch_shapes=[pltpu.VMEM((BRANCH_WIDTH, BRANCH_WIDTH), F32)],
        compiler_params=_params("arbitrary"),
        name="deltanet",
    )(ga, alog, dtb, norm)


def _gla_kernel(c_ref, norm_ref, o_ref, state_ref, *, tb, blocks_per_seq):
    i = pl.program_id(0)

    @pl.when(i % blocks_per_seq == 0)
    def _():
        state_ref[...] = jnp.zeros_like(state_ref)

    qw = HEADS * GLA_DK
    bw = BRANCH_WIDTH
    nchunk = tb // CHUNK
    la_off = 2 * qw + 2 * bw
    kshift = GLA_DK.bit_length() - 1
    vshift = HEAD_DIM.bit_length() - 1
    r64 = lax.broadcasted_iota(jnp.int32, (CHUNK, CHUNK), 0)
    c64 = lax.broadcasted_iota(jnp.int32, (CHUNK, CHUNK), 1)
    lmat = _bf(jnp.where(c64 <= r64, 1.0, 0.0))
    t_idx = lax.broadcasted_iota(jnp.int32, (CHUNK, bw), 0)
    lane_v = lax.broadcasted_iota(jnp.int32, (CHUNK, bw), 1)
    lane_k = lax.broadcasted_iota(jnp.int32, (CHUNK, qw), 1)
    lower = (lane_v & (HEAD_DIM - 1)) <= t_idx
    sel_v = [_bf(jnp.where((lane_v >> vshift) == h, 1.0, 0.0)) for h in range(HEADS)]
    sel_k = [_bf(jnp.where((lane_k >> kshift) == h, 1.0, 0.0)) for h in range(HEADS)]
    same_head = (lax.broadcasted_iota(jnp.int32, (bw, qw), 0) >> vshift) == \
                (lax.broadcasted_iota(jnp.int32, (bw, qw), 1) >> kshift)

    def bd(y, sel):
        return jnp.concatenate([y * m for m in sel], axis=0)

    chunks = range(nchunk)
    rows = lambda col0, width, c: c_ref[c * CHUNK:(c + 1) * CHUNK, col0:col0 + width]

    bcums = []
    for c in chunks:
        a1, a2 = _split(rows(la_off, qw, c), 2)
        bcums.append(_dot(lmat, a1) + _dot(lmat, a2))
    q_ins = [_bf(rows(0, qw, c) * (GLA_DK ** -0.5) * jnp.exp(b)) for c, b in zip(chunks, bcums)]
    k_ins = [_bf(rows(qw, qw, c) * jnp.exp(-b)) for c, b in zip(chunks, bcums)]
    vbs = [_bf(rows(2 * qw, bw, c)) for c in chunks]
    attns = [_bf(jnp.where(lower, _dot_nt(qi, bd(ki, sel_k)), 0.0)) for qi, ki in zip(q_ins, k_ins)]
    intras = [_dot(a, bd(vb, sel_v)) for a, vb in zip(attns, vbs)]
    lasts = [b[CHUNK - 1:CHUNK, :] for b in bcums]
    k_tails = [_bf(rows(qw, qw, c) * jnp.exp(l - b)) for c, l, b in zip(chunks, lasts, bcums)]
    cds = [jnp.exp(l) for l in lasts]

    st = state_ref[...]
    o_rows = []
    for c in chunks:
        o_rows.append(intras[c] + _dot_nt(q_ins[c], _bf(st)))
        st = st * cds[c] + jnp.where(same_head, _dot_tn(vbs[c], k_tails[c]), 0.0)
    state_ref[...] = st
    o_all = jnp.concatenate(o_rows, axis=0)
    gate = c_ref[:, 2 * qw + bw:2 * qw + 2 * bw]
    o_ref[...] = _bf(_head_rms(o_all, _head_ones(), norm_ref[...]) * _silu(gate))


def _gla(gc, norm, seq, tb=512):
    t = gc.shape[0]
    return pl.pallas_call(
        functools.partial(_gla_kernel, tb=tb, blocks_per_seq=seq // tb),
        grid=(t // tb,),
        in_specs=[pl.BlockSpec((tb, C_WIDTH), lambda i: (i, 0)), _const_spec((1, BRANCH_WIDTH))],
        out_specs=pl.BlockSpec((tb, BRANCH_WIDTH), lambda i: (i, 0)),
        out_shape=jax.ShapeDtypeStruct((t, BRANCH_WIDTH), BF16),
        scratch_shapes=[pltpu.VMEM((BRANCH_WIDTH, HEADS * GLA_DK), F32)],
        compiler_params=_params("arbitrary"),
        name="gla",
    )(gc, norm)


def _mla_prep_kernel(b_ref, qn_ref, kvn_ref, wq_ref, wkv_ref, cos_ref, sin_ref,
                     q_ref, k_ref, v_ref):
    cq = b_ref[:, 0:MLA_Q_LORA]
    ckv = b_ref[:, MLA_Q_LORA:MLA_Q_LORA + MLA_KV_LORA]
    off = MLA_Q_LORA + MLA_KV_LORA
    kpe = b_ref[:, off:off + SLOT]
    kpe_rot = b_ref[:, off + SLOT:off + 2 * SLOT]
    cos = cos_ref[...]
    sin = sin_ref[...]

    qa = _dot(_bf(_rms(cq, qn_ref[...])), wq_ref[...])
    kv = _dot(_bf(_rms(ckv, kvn_ref[...])), wkv_ref[...])
    k_pe = kpe * cos + kpe_rot * sin
    nslot = HEADS * SLOT
    for h in range(HEADS):
        sl = slice(h * SLOT, (h + 1) * SLOT)
        q_rope = qa[:, sl] * cos + qa[:, nslot + h * SLOT:nslot + (h + 1) * SLOT] * sin
        q_ref[:, sl] = _bf(q_rope * ((MLA_NOPE + MLA_ROPE) ** -0.5 * LOG2E))
        k_ref[:, sl] = _bf(kv[:, sl] + k_pe)
    v_ref[...] = _bf(kv[:, nslot:nslot + BRANCH_WIDTH])


def _mla_prep(gb, qn, kvn, wq, wkv, cos_t, sin_t, seq, tm=512):
    t = gb.shape[0]
    tps = seq // tm
    row = lambda w: pl.BlockSpec((tm, w), lambda i: (i, 0))
    pos = pl.BlockSpec((tm, SLOT), lambda i: (i % tps, 0))
    nslot = HEADS * SLOT
    return pl.pallas_call(
        _mla_prep_kernel,
        grid=(t // tm,),
        in_specs=[row(B_WIDTH), _const_spec((1, MLA_Q_LORA)), _const_spec((1, MLA_KV_LORA)),
                  _const_spec((MLA_Q_LORA, 2 * nslot)), _const_spec((MLA_KV_LORA, nslot + BRANCH_WIDTH)),
                  pos, pos],
        out_specs=[row(nslot), row(nslot), row(BRANCH_WIDTH)],
        out_shape=[jax.ShapeDtypeStruct((t, nslot), BF16), jax.ShapeDtypeStruct((t, nslot), BF16),
                   jax.ShapeDtypeStruct((t, BRANCH_WIDTH), BF16)],
        compiler_params=_params("parallel"),
        name="mla_prep",
    )(gb, qn, kvn, wq, wkv, cos_t, sin_t)


def _fox_prep_kernel(d_ref, fb_ref, q_ref, k_ref, v_ref, carry_ref, *, tb, blocks_per_seq):
    i = pl.program_id(0)

    @pl.when(i % blocks_per_seq == 0)
    def _():
        carry_ref[...] = jnp.zeros_like(carry_ref)

    bw = BRANCH_WIDTH
    log_f = _log_sigmoid(d_ref[:, 3 * bw:3 * bw + LANE] + fb_ref[...])
    lmat = _bf(jnp.where(_tri(tb), 1.0, 0.0))
    fcum = _dot_ones(lmat, log_f) + carry_ref[...]
    carry_ref[...] = fcum[tb - 1:tb, :]
    f1, f2, f3 = [f.astype(F32) for f in _split(fcum * LOG2E, 3)]

    lane = lax.broadcasted_iota(jnp.int32, (tb, LANE), 1)
    scale = HEAD_DIM ** -0.5 * LOG2E
    for h in range(HEADS):
        pair, odd = divmod(h, 2)
        qp = d_ref[:, pair * LANE:(pair + 1) * LANE] * scale
        kp = d_ref[:, bw + pair * LANE:bw + (pair + 1) * LANE]
        head_lanes = (lane >= HEAD_DIM) if odd else (lane < HEAD_DIM)
        a0 = 0 if odd else HEAD_DIM
        fb = [jnp.broadcast_to(f[:, h:h + 1], (tb, LANE)) for f in (f1, f2, f3)]
        qx = jnp.zeros((tb, LANE), F32)
        kx = jnp.zeros((tb, LANE), F32)
        for j in range(3):
            qx = jnp.where(lane == a0 + j, fb[j], qx)
            qx = jnp.where(lane == a0 + 3 + j, 1.0, qx)
            kx = jnp.where(lane == a0 + j, 1.0, kx)
            kx = jnp.where(lane == a0 + 3 + j, -fb[j], kx)
        q_ref[:, h * SLOT:(h + 1) * SLOT] = _bf(jnp.where(head_lanes, qp, qx))
        k_ref[:, h * SLOT:(h + 1) * SLOT] = _bf(jnp.where(head_lanes, kp, kx))
    v_ref[...] = _bf(d_ref[:, 2 * bw:3 * bw])


def _fox_prep(gd, fbias, seq, tb=256):
    t = gd.shape[0]
    nslot = HEADS * SLOT
    row = lambda w: pl.BlockSpec((tb, w), lambda i: (i, 0))
    return pl.pallas_call(
        functools.partial(_fox_prep_kernel, tb=tb, blocks_per_seq=seq // tb),
        grid=(t // tb,),
        in_specs=[row(D_WIDTH), _const_spec((1, LANE))],
        out_specs=[row(nslot), row(nslot), row(BRANCH_WIDTH)],
        out_shape=[jax.ShapeDtypeStruct((t, nslot), BF16), jax.ShapeDtypeStruct((t, nslot), BF16),
                   jax.ShapeDtypeStruct((t, BRANCH_WIDTH), BF16)],
        scratch_shapes=[pltpu.VMEM((1, LANE), F32)],
        compiler_params=_params("arbitrary"),
        name="fox_prep",
    )(gd, fbias)


VT_ROWS = HEAD_DIM + 16


def _flash_kernel(q_ref, k_ref, vt_ref, o_ref, m_ref, acc_ref, s_ref, *, chunk_mask, tq, tk):
    assert tq == tk
    qi = pl.program_id(1)
    m_ref[...] = jnp.full_like(m_ref, -jnp.inf)
    acc_ref[...] = jnp.zeros_like(acc_ref)

    def scores(ki, slot, hh):
        s_ref[slot, hh] = _dot_nt(k_ref[ki, :, hh * SLOT:(hh + 1) * SLOT],
                                  q_ref[:, hh * SLOT:(hh + 1) * SLOT])

    def softmax_pv(ki, slot, hh, masked):
        s = s_ref[slot, hh]
        if masked:
            kpos = lax.broadcasted_iota(jnp.int32, s.shape, 0)
            qpos = lax.broadcasted_iota(jnp.int32, s.shape, 1)
            if chunk_mask:
                shift = CHUNK.bit_length() - 1
                ok = (kpos >> shift) <= (qpos >> shift)
            else:
                ok = kpos <= qpos
            s = jnp.where(ok, s, -jnp.inf)
        m_prev = m_ref[hh]
        m_new = jnp.maximum(m_prev, jnp.max(s, axis=0, keepdims=True))
        p = _bf(jnp.exp2(s - m_new))
        pv = _dot(vt_ref[ki, hh * VT_ROWS:(hh + 1) * VT_ROWS, :], p)
        acc_ref[hh] = jnp.exp2(m_prev - m_new) * acc_ref[hh] + pv
        m_ref[hh] = m_new

    def step(ki, slot, masked=False, prefetch=True):
        for hh in range(HEADS):
            if prefetch:
                scores(ki + 1, 1 - slot, hh)
            softmax_pv(ki, slot, hh, masked)

    for hh in range(HEADS):
        scores(0, 0, hh)

    def pair(j, carry):
        step(2 * j, 0)
        step(2 * j + 1, 1)
        return carry

    lax.fori_loop(0, qi // 2, pair, 0)

    @pl.when(qi % 2 == 0)
    def _():
        step(qi, 0, masked=True, prefetch=False)

    @pl.when(qi % 2 == 1)
    def _():
        step(qi - 1, 0)
        step(qi, 1, masked=True, prefetch=False)

    ot = jnp.concatenate([acc_ref[hh][0:HEAD_DIM] / acc_ref[hh][HEAD_DIM:HEAD_DIM + 1]
                          for hh in range(HEADS)], axis=0)
    o_ref[...] = _bf(ot.T)


def _flash(q, k, v, batch, seq, chunk_mask, tq=512, tk=512):
    nq = seq // tq
    nk = seq // tk
    t = q.shape[0]
    k3 = k.reshape(t // tk, tk, HEADS * SLOT)
    vt = v.reshape(t // tk, tk, HEADS, HEAD_DIM).transpose(0, 2, 3, 1)
    ones = jnp.ones((t // tk, HEADS, 1, tk), BF16)
    zeros = jnp.zeros((t // tk, HEADS, VT_ROWS - HEAD_DIM - 1, tk), BF16)
    vt3 = jnp.concatenate([vt, ones, zeros], axis=2).reshape(t // tk, HEADS * VT_ROWS, tk)
    return pl.pallas_call(
        functools.partial(_flash_kernel, chunk_mask=chunk_mask, tq=tq, tk=tk),
        grid=(batch, nq),
        in_specs=[pl.BlockSpec((tq, HEADS * SLOT), lambda b, qi: (b * nq + qi, 0)),
                  pl.BlockSpec((nk, tk, HEADS * SLOT), lambda b, qi: (b, 0, 0)),
                  pl.BlockSpec((nk, HEADS * VT_ROWS, tk), lambda b, qi: (b, 0, 0))],
        out_specs=pl.BlockSpec((tq, BRANCH_WIDTH), lambda b, qi: (b * nq + qi, 0)),
        out_shape=jax.ShapeDtypeStruct((t, BRANCH_WIDTH), BF16),
        scratch_shapes=[pltpu.VMEM((HEADS, 1, tq), F32), pltpu.VMEM((HEADS, VT_ROWS, tq), F32),
                        pltpu.VMEM((2, HEADS, tk, tq), F32)],
        compiler_params=_params("parallel", "arbitrary"),
        name="flash_chunk" if chunk_mask else "flash_causal",
    )(q, k3, vt3)


def _merge_kernel(x_ref, oa_ref, ob_ref, oc_ref, od_ref, gpre_ref, wg_ref, bg_ref, wup_ref,
                  wout_ref, gpost_ref, o_ref):
    x = x_ref[...]
    h = _bf(_rms(x, gpre_ref[...]))
    merged = jnp.zeros(x.shape, F32)
    for i, br in enumerate((oa_ref, ob_ref, oc_ref, od_ref)):
        gate = jax.nn.sigmoid(_dot(h, wg_ref[i]) + bg_ref[i:i + 1, :])
        merged = merged + gate * _dot(br[...], wup_ref[i])
    y = _dot(_bf(merged), wout_ref[...])
    o_ref[...] = x + _rms(y, gpost_ref[...])


def _merge(x2, oa, ob, oc, od, gpre, wg, bg, wup, wout, gpost, tm=512):
    t = x2.shape[0]
    row = lambda w: pl.BlockSpec((tm, w), lambda i: (i, 0))
    return pl.pallas_call(
        _merge_kernel,
        grid=(t // tm,),
        in_specs=[row(D_MODEL)] + [row(BRANCH_WIDTH)] * 4 +
                 [_const_spec((1, D_MODEL)), _const_spec((4, D_MODEL, D_MODEL)),
                  _const_spec((4, D_MODEL)), _const_spec((4, BRANCH_WIDTH, D_MODEL)),
                  _const_spec((D_MODEL, D_MODEL)), _const_spec((1, D_MODEL))],
        out_specs=row(D_MODEL),
        out_shape=jax.ShapeDtypeStruct((t, D_MODEL), F32),
        compiler_params=_params("parallel"),
        name="merge",
    )(x2, oa, ob, oc, od, gpre, wg, bg, wup, wout, gpost)


def _mlp_kernel(x_ref, gpre_ref, w1_ref, w2_ref, gpost_ref, o_ref, *, ff_chunk):
    x = x_ref[...]
    h = _bf(_rms(x, gpre_ref[...]))
    y = jnp.zeros(x.shape, F32)
    for c in range(D_FF // ff_chunk):
        u = jnp.maximum(_dot(h, w1_ref[:, c * ff_chunk:(c + 1) * ff_chunk]), 0.0)
        y = y + _dot(_bf(u * u), w2_ref[c * ff_chunk:(c + 1) * ff_chunk, :])
    o_ref[...] = x + _rms(y, gpost_ref[...])


def _mlp(x2, gpre, w1, w2, gpost, tm=512, ff_chunk=1024):
    t = x2.shape[0]
    row = pl.BlockSpec((tm, D_MODEL), lambda i: (i, 0))
    return pl.pallas_call(
        functools.partial(_mlp_kernel, ff_chunk=ff_chunk),
        grid=(t // tm,),
        in_specs=[row, _const_spec((1, D_MODEL)), _const_spec((D_MODEL, D_FF)),
                  _const_spec((D_FF, D_MODEL)), _const_spec((1, D_MODEL))],
        out_specs=row,
        out_shape=jax.ShapeDtypeStruct((t, D_MODEL), F32),
        compiler_params=_params("parallel"),
        name="mlp",
    )(x2, gpre, w1, w2, gpost)


_IN_SIZES = (
    HEADS * 3 * HEAD_DIM, HEADS * HEAD_DIM, HEADS, HEADS,
    MLA_Q_LORA, MLA_KV_LORA, MLA_ROPE,
    HEADS * GLA_DK, HEADS * GLA_DK, HEADS * HEAD_DIM, HEADS * HEAD_DIM, GLA_RANK,
    HEADS * HEAD_DIM, HEADS * HEAD_DIM, HEADS * HEAD_DIM, HEADS,
)


def _rot_cols(w):
    half = w.shape[-1] // 2
    return jnp.concatenate([-w[..., half:], w[..., :half]], axis=-1)


def _pad_cols(w, width):
    return jnp.pad(w, ((0, 0), (0, width - w.shape[-1])))


def _layout_in_weight(w_in):
    splits = np.cumsum(_IN_SIZES)[:-1].tolist()
    (dn_qkv, dn_z, dn_b, dn_a, cq, ckv, kpe, gq, gk, gv, gg, glr, fq, fk, fv, ff) = jnp.split(w_in, splits, axis=1)
    zeros = lambda n: jnp.zeros((w_in.shape[0], n), w_in.dtype)
    pe_slot = lambda w: jnp.concatenate([zeros(MLA_NOPE), w, zeros(SLOT - MLA_NOPE - MLA_ROPE)], axis=1)
    cols = [dn_qkv, dn_z, jnp.repeat(dn_a, HEAD_DIM, axis=1), jnp.repeat(dn_b, HEAD_DIM, axis=1),
            cq, ckv, pe_slot(kpe), pe_slot(_rot_cols(kpe)),
            gq, gk, gv, gg, _pad_cols(glr, LANE),
            fq, fk, fv, _pad_cols(ff, LANE)]
    w = jnp.concatenate(cols, axis=1)
    assert w.shape[1] == IN_TOTAL
    return _bf(w)


def _layout_q_up(w):
    zeros = lambda n: jnp.zeros((w.shape[0], n), w.dtype)
    per = MLA_NOPE + MLA_ROPE
    plain, rot = [], []
    for h in range(HEADS):
        nope = w[:, h * per:h * per + MLA_NOPE]
        pe = w[:, h * per + MLA_NOPE:(h + 1) * per]
        plain += [nope, pe, zeros(SLOT - per)]
        rot += [zeros(MLA_NOPE), _rot_cols(pe), zeros(SLOT - per)]
    return _bf(jnp.concatenate(plain + rot, axis=1))


def _layout_kv_up(w):
    zeros = lambda n: jnp.zeros((w.shape[0], n), w.dtype)
    per = MLA_NOPE + HEAD_DIM
    ks, vs = [], []
    for h in range(HEADS):
        ks += [w[:, h * per:h * per + MLA_NOPE], zeros(SLOT - MLA_NOPE)]
        vs += [w[:, h * per + MLA_NOPE:(h + 1) * per]]
    return _bf(jnp.concatenate(ks + vs, axis=1))


def _rope_tables(seq):
    inv_freq = 1.0 / (ROPE_THETA ** (jnp.arange(0, MLA_ROPE, 2, dtype=F32) / MLA_ROPE))
    ang = jnp.arange(seq, dtype=F32)[:, None] * inv_freq[None, :]
    ang = jnp.concatenate([ang, ang], axis=-1)
    pad = jnp.zeros((seq, SLOT - MLA_NOPE - MLA_ROPE), F32)
    cos_t = jnp.concatenate([jnp.ones((seq, MLA_NOPE), F32), jnp.cos(ang), pad], axis=1)
    sin_t = jnp.concatenate([jnp.zeros((seq, MLA_NOPE), F32), jnp.sin(ang), pad], axis=1)
    return cos_t, sin_t


def _layer(x2, batch, seq, p):
    ga, gb, gc, gd = _inproj(x2, p["norm_mix_pre"], p["w_cat"], p["dn_conv"], p["wgu"], p["gbias"], seq)
    o_a = _deltanet(ga, p["alog"], p["dtb"], p["dn_norm"], seq)
    q_b, k_b, v_b = _mla_prep(gb, p["mla_q_norm"], p["mla_kv_norm"], p["wq"], p["wkv"],
                              p["cos"], p["sin"], seq)
    o_b = _flash(q_b, k_b, v_b, batch, seq, True)
    o_c = _gla(gc, p["gla_norm"], seq)
    q_d, k_d, v_d = _fox_prep(gd, p["fbias"], seq)
    o_d = _flash(q_d, k_d, v_d, batch, seq, False)
    x2 = _merge(x2, o_a, o_b, o_c, o_d, p["norm_mix_pre"], p["w_gate"], p["b_gate"],
                p["w_branch_up"], p["w_out"], p["norm_mix_post"])
    return _mlp(x2, p["norm_mlp_pre"], p["w_mlp_in"], p["w_mlp_out"], p["norm_mlp_post"])


def kernel(x, norm_mix_pre, norm_mix_post, norm_mlp_pre, norm_mlp_post, w_in, dn_conv, dn_a_log, dn_dt_bias, dn_norm, mla_q_norm, mla_w_q_up, mla_kv_norm, mla_w_kv_up, gla_w_gate_up, gla_gate_bias, gla_norm, fox_f_bias, w_branch_up, w_gate, b_gate, w_out, w_mlp_in, w_mlp_out):
    batch, seq, d = x.shape
    depth = w_in.shape[0]
    cos_t, sin_t = _rope_tables(seq)
    x2 = x.reshape(batch * seq, d)
    row = lambda a: a.reshape(1, -1)
    for l in range(depth):
        p = {
            "norm_mix_pre": row(norm_mix_pre[l]), "norm_mix_post": row(norm_mix_post[l]),
            "norm_mlp_pre": row(norm_mlp_pre[l]), "norm_mlp_post": row(norm_mlp_post[l]),
            "w_cat": _layout_in_weight(w_in[l]),
            "dn_conv": dn_conv[l],
            "wgu": _bf(jnp.pad(gla_w_gate_up[l], ((0, LANE - GLA_RANK), (0, 0)))),
            "gbias": row(gla_gate_bias[l]),
            "alog": row(jnp.repeat(dn_a_log[l], HEAD_DIM)), "dtb": row(jnp.repeat(dn_dt_bias[l], HEAD_DIM)),
            "dn_norm": row(jnp.tile(dn_norm[l], HEADS)),
            "mla_q_norm": row(mla_q_norm[l]), "mla_kv_norm": row(mla_kv_norm[l]),
            "wq": _layout_q_up(mla_w_q_up[l]), "wkv": _layout_kv_up(mla_w_kv_up[l]),
            "cos": cos_t, "sin": sin_t,
            "gla_norm": row(jnp.tile(gla_norm[l], HEADS)),
            "fbias": row(jnp.pad(fox_f_bias[l], (0, LANE - HEADS))),
            "w_gate": _bf(w_gate[l]), "b_gate": b_gate[l], "w_branch_up": _bf(w_branch_up[l]),
            "w_out": _bf(w_out[l]), "w_mlp_in": _bf(w_mlp_in[l]), "w_mlp_out": _bf(w_mlp_out[l]),
        }
        x2 = _layer(x2, batch, seq, p)
    return x2.reshape(batch, seq, d)
```

```python
import functools
import math

import jax
import jax.numpy as jnp
import numpy as np
from jax import lax
from jax.experimental import pallas as pl
from jax.experimental.pallas import tpu as pltpu

F32 = jnp.float32
BF16 = jnp.bfloat16

D_MODEL = 1024
CHUNK = 64
EPS = 1e-6
HEADS = 4
HEAD_DIM = 64
BRANCH_WIDTH = HEADS * HEAD_DIM
D_FF = 4 * D_MODEL
CONV_K = 4
MLA_NOPE = HEAD_DIM
MLA_ROPE = HEAD_DIM // 2
MLA_Q_LORA = D_MODEL // 4
MLA_KV_LORA = D_MODEL // 8
ROPE_THETA = 10000.0
GLA_DK = HEAD_DIM // 2
GLA_RANK = 16
GLA_TAU = 16.0
LOG2E = math.log2(math.e)
LANE = 128
SLOT = LANE

A_WIDTH = 3 * BRANCH_WIDTH + 3 * BRANCH_WIDTH
B_WIDTH = MLA_Q_LORA + MLA_KV_LORA + 2 * SLOT
C_WIDTH = 2 * HEADS * GLA_DK + 2 * BRANCH_WIDTH + LANE
D_WIDTH = 3 * BRANCH_WIDTH + LANE
IN_TOTAL = A_WIDTH + B_WIDTH + C_WIDTH + D_WIDTH

VMEM_LIMIT = 56 * 1024 * 1024


def _dot(a, b):
    return jnp.dot(a, b, preferred_element_type=F32)


def _dot_nt(a, b):
    return lax.dot_general(a, b, (((1,), (1,)), ((), ())), preferred_element_type=F32)


def _dot_tn(a, b):
    return lax.dot_general(a, b, (((0,), (0,)), ((), ())), preferred_element_type=F32)


def _bf(x):
    return x.astype(BF16)


def _split(x, pieces):
    out = []
    for _ in range(pieces - 1):
        hi = _bf(x)
        out.append(hi)
        x = x - hi.astype(F32)
    out.append(_bf(x))
    return out


def _dot3(a, b):
    ah, al = _split(a, 2)
    bh, bl = _split(b, 2)
    return _dot(ah, bh) + (_dot(ah, bl) + _dot(al, bh))


def _dot_ones(ones_bf, x):
    x1, x2, x3 = _split(x, 3)
    return _dot(ones_bf, x1) + (_dot(ones_bf, x2) + _dot(ones_bf, x3))


def _rms(x, gain):
    return x * lax.rsqrt(jnp.mean(x * x, axis=-1, keepdims=True) + EPS) * gain


def _silu(x):
    return x * jax.nn.sigmoid(x)


def _softplus(x):
    return jnp.maximum(x, 0.0) + jnp.log1p(jnp.exp(-jnp.abs(x)))


def _log_sigmoid(x):
    return jnp.minimum(x, 0.0) - jnp.log1p(jnp.exp(-jnp.abs(x)))


def _tri(n, strict=False):
    r = lax.broadcasted_iota(jnp.int32, (n, n), 0)
    c = lax.broadcasted_iota(jnp.int32, (n, n), 1)
    return (c < r) if strict else (c <= r)


def _params(*sem):
    return pltpu.CompilerParams(dimension_semantics=sem, vmem_limit_bytes=VMEM_LIMIT)


def _const_spec(shape):
    nd = len(shape)
    return pl.BlockSpec(shape, lambda *_: (0,) * nd, pipeline_mode=pl.Buffered(1))


def _inproj_kernel(x_ref, gain_ref, w_ref, convw_ref, wgu_ref, gbias_ref,
                   qn_ref, kvn_ref, wq_ref, wkv_ref, cos_ref, sin_ref, fb_ref,
                   oa_ref, oc_ref, qb_ref, kb_ref, vtb_ref, qd_ref, kd_ref, vtd_ref,
                   xe_ref, carry_ref, *, tm, tiles_per_seq):
    i = pl.program_id(0)
    h = _bf(_rms(x_ref[...], gain_ref[...]))

    qkv_w = 3 * BRANCH_WIDTH
    cw = BRANCH_WIDTH

    @pl.when(i % tiles_per_seq == 0)
    def _():
        xe_ref[0:8, :] = jnp.zeros((8, qkv_w), F32)
        carry_ref[...] = jnp.zeros_like(carry_ref)

    def conv_cols(c0):
        y = jnp.zeros((tm, cw), F32)
        for j in range(CONV_K):
            y = y + convw_ref[j:j + 1, c0:c0 + cw] * xe_ref[pl.ds(8 - (CONV_K - 1) + j, tm), c0:c0 + cw]
        oa_ref[:, c0:c0 + cw] = _silu(y)
        xe_ref[0:8, c0:c0 + cw] = xe_ref[tm:tm + 8, c0:c0 + cw]

    for c0 in range(0, qkv_w, cw):
        xe_ref[8:8 + tm, c0:c0 + cw] = _dot(h, w_ref[:, c0:c0 + cw])
        if c0:
            conv_cols(c0 - cw)
    oa_ref[:, qkv_w:A_WIDTH] = _dot(h, w_ref[:, qkv_w:A_WIDTH])
    conv_cols(qkv_w - cw)

    off_c = A_WIDTH + B_WIDTH
    off_d = off_c + C_WIDTH
    lr_off = C_WIDTH - LANE
    pb = _dot(h, w_ref[:, A_WIDTH:A_WIDTH + B_WIDTH])
    pd = _dot(h, w_ref[:, off_d:off_d + D_WIDTH])
    _mla_operands(pb, qn_ref, kvn_ref, wq_ref, wkv_ref, cos_ref, sin_ref, qb_ref, kb_ref, vtb_ref)
    lr = _dot(h, w_ref[:, off_c + lr_off:off_c + C_WIDTH])
    oc_ref[:, 0:lr_off] = _dot(h, w_ref[:, off_c:off_c + lr_off])
    _fox_operands(pd, fb_ref, carry_ref, qd_ref, kd_ref, vtd_ref)
    gate = _dot(_bf(lr), wgu_ref[...]) + gbias_ref[...]
    oc_ref[:, lr_off:C_WIDTH] = _log_sigmoid(gate) * (1.0 / GLA_TAU)


def _inproj(x2, p, seq, tm):
    t = x2.shape[0]
    tps = seq // tm
    nslot = HEADS * SLOT
    row = lambda w: pl.BlockSpec((tm, w), lambda i: (i, 0))
    pos = pl.BlockSpec((tm, SLOT), lambda i: (i % tps, 0))
    vt_spec = pl.BlockSpec((1, HEADS * VT_ROWS, tm), lambda i: (i, 0, 0))
    slots = jax.ShapeDtypeStruct((t, nslot), BF16)
    vts = jax.ShapeDtypeStruct((t // tm, HEADS * VT_ROWS, tm), BF16)
    return pl.pallas_call(
        functools.partial(_inproj_kernel, tm=tm, tiles_per_seq=tps),
        grid=(t // tm,),
        in_specs=[row(D_MODEL), _const_spec((1, D_MODEL)), _const_spec((D_MODEL, IN_TOTAL)),
                  _const_spec((CONV_K, 3 * BRANCH_WIDTH)), _const_spec((LANE, LANE)),
                  _const_spec((1, LANE)),
                  _const_spec((1, MLA_Q_LORA)), _const_spec((1, MLA_KV_LORA)),
                  _const_spec((MLA_Q_LORA, 2 * nslot)), _const_spec((MLA_KV_LORA, nslot + BRANCH_WIDTH)),
                  pos, pos, _const_spec((1, LANE))],
        out_specs=[row(A_WIDTH), row(C_WIDTH), row(nslot), row(nslot), vt_spec,
                   row(nslot), row(nslot), vt_spec],
        out_shape=[jax.ShapeDtypeStruct((t, A_WIDTH), F32), jax.ShapeDtypeStruct((t, C_WIDTH), F32),
                   slots, slots, vts, slots, slots, vts],
        scratch_shapes=[pltpu.VMEM((tm + 8, 3 * BRANCH_WIDTH), F32), pltpu.VMEM((1, LANE), F32)],
        compiler_params=_params("arbitrary"),
        name="inproj",
    )(x2, p["norm_mix_pre"], p["w_cat"], p["dn_conv"], p["wgu"], p["gbias"],
      p["mla_q_norm"], p["mla_kv_norm"], p["wq"], p["wkv"], p["cos"], p["sin"], p["fbias"])


def _head_ones():
    r = lax.broadcasted_iota(jnp.int32, (BRANCH_WIDTH, BRANCH_WIDTH), 0)
    c = lax.broadcasted_iota(jnp.int32, (BRANCH_WIDTH, BRANCH_WIDTH), 1)
    shift = HEAD_DIM.bit_length() - 1
    return _bf(jnp.where((r >> shift) == (c >> shift), 1.0, 0.0))


def _head_sums(x, ones_bd):
    x1, x2 = _split(x, 2)
    return _dot(x1, ones_bd) + _dot(x2, ones_bd)


def _head_rms(x, ones_bd, gain):
    return x * lax.rsqrt(_head_sums(x * x, ones_bd) * (1.0 / HEAD_DIM) + EPS) * gain


def _deltanet_kernel(a_ref, alog_ref, dtb_ref, norm_ref, o_ref, state_ref, *, tb, blocks_per_seq):
    i = pl.program_id(0)

    @pl.when(i % blocks_per_seq == 0)
    def _():
        state_ref[...] = jnp.zeros_like(state_ref)

    bw = BRANCH_WIDTH
    nchunk = tb // CHUNK
    hshift = HEAD_DIM.bit_length() - 1
    t_idx = lax.broadcasted_iota(jnp.int32, (CHUNK, bw), 0)
    lane = lax.broadcasted_iota(jnp.int32, (CHUNK, bw), 1)
    s_idx = lane & (HEAD_DIM - 1)
    lower = s_idx <= t_idx
    strict = s_idx < t_idx
    same_block = lambda log2s: (t_idx >> log2s) == (s_idx >> log2s)
    eye = jnp.where(s_idx == t_idx, 1.0, 0.0).astype(F32)
    strict_f = jnp.where(strict, 1.0, 0.0).astype(F32)
    head_sel = [_bf(jnp.where((lane >> hshift) == h, 1.0, 0.0)) for h in range(HEADS)]
    r64 = lax.broadcasted_iota(jnp.int32, (CHUNK, CHUNK), 0)
    c64 = lax.broadcasted_iota(jnp.int32, (CHUNK, CHUNK), 1)
    lmat = _bf(jnp.where(c64 <= r64, 1.0, 0.0))
    r256 = lax.broadcasted_iota(jnp.int32, (bw, bw), 0)
    c256 = lax.broadcasted_iota(jnp.int32, (bw, bw), 1)
    same_head = (r256 >> hshift) == (c256 >> hshift)
    ones_bd = _bf(jnp.where(same_head, 1.0, 0.0))

    def bd(y):
        return jnp.concatenate([y * m for m in head_sel], axis=0)

    def mm(a, b):
        return _dot(a, bd(b))

    def mm3(a, b):
        ah, al = _split(a, 2)
        bh, bl = _split(b, 2)
        bdh = bd(bh)
        return _dot(ah, bdh) + (_dot(ah, bd(bl)) + _dot(al, bdh))

    neg_a = -jnp.exp(alog_ref[...])
    g_all = neg_a * _softplus(a_ref[:, 4 * bw:5 * bw] + dtb_ref[...])
    beta_all = jax.nn.sigmoid(a_ref[:, 5 * bw:6 * bw])
    q_all = a_ref[:, 0:bw]
    k_all = a_ref[:, bw:2 * bw]
    q_all = q_all * lax.rsqrt(_head_sums(q_all * q_all, ones_bd) + EPS) * (HEAD_DIM ** -0.5)
    k_all = k_all * lax.rsqrt(_head_sums(k_all * k_all, ones_bd) + EPS)
    vb_all = a_ref[:, 2 * bw:3 * bw] * beta_all

    chunks = range(nchunk)
    rows = lambda x, c: x[c * CHUNK:(c + 1) * CHUNK, :]
    qs = [rows(q_all, c) for c in chunks]
    ks = [rows(k_all, c) for c in chunks]
    gs = [rows(g_all, c) for c in chunks]
    betas = [rows(beta_all, c) for c in chunks]

    cums = []
    for g in gs:
        g1, g2 = _split(jnp.concatenate([g * strict_f, g], axis=1), 2)
        cums.append(_dot(lmat, g1) + _dot(lmat, g2))
    diffs = [cm[:, 0:bw] for cm in cums]
    gcs = [cm[:, bw:2 * bw] for cm in cums]
    decays = [jnp.where(lower, jnp.exp(jnp.where(lower, d, 0.0)), 0.0) for d in diffs]
    kbs = [_bf(k) for k in ks]
    kqs = [_dot_nt(jnp.concatenate([kb, _bf(q)], axis=0), bd(kb)) for kb, q in zip(kbs, qs)]
    amats = [jnp.where(strict, b * kq[0:CHUNK] * d, 0.0) for b, kq, d in zip(betas, kqs, decays)]

    xs = [eye - jnp.where(same_block(1), a, 0.0) for a in amats]
    for log2s in range(1, CHUNK.bit_length() - 1):
        off = same_block(log2s + 1) & ~same_block(log2s)
        ns = [_bf(jnp.where(off, a, 0.0)) for a in amats]
        xbs = [_bf(x) for x in xs]
        xns = [_bf(mm(xb, n)) for xb, n in zip(xbs, ns)]
        xs = [x - mm(xn, xb) for x, xn, xb in zip(xs, xns, xbs)]
    resid = [eye - mm3(eye + a, x) for a, x in zip(amats, xs)]
    xs = [x + mm(_bf(x), _bf(r)) for x, r in zip(xs, resid)]

    egcs = [jnp.exp(gc) for gc in gcs]
    us = [mm3(x, rows(vb_all, c)) for x, c in zip(xs, chunks)]
    wbs = [_bf(mm(_bf(x), _bf(k * b * e))) for x, k, b, e in zip(xs, ks, betas, egcs)]
    qks = [_bf(kq[CHUNK:2 * CHUNK] * d) for kq, d in zip(kqs, decays)]
    q_decays = [_bf(q * e) for q, e in zip(qs, egcs)]
    lasts = [gc[CHUNK - 1:CHUNK, :] for gc in gcs]
    k_tails = [_bf(k * jnp.exp(l - gc)) for k, l, gc in zip(ks, lasts, gcs)]
    cds = [jnp.exp(l) for l in lasts]

    s = state_ref[...]
    o_rows = []
    for c in chunks:
        sb = _bf(s)
        vb = _bf(us[c] - _dot(wbs[c], sb))
        o_rows.append(_dot(q_decays[c], sb) + mm(qks[c], vb))
        s = s * cds[c] + jnp.where(same_head, _dot_tn(k_tails[c], vb), 0.0)
    state_ref[...] = s
    o_all = jnp.concatenate(o_rows, axis=0)
    o_ref[...] = _bf(_head_rms(o_all, ones_bd, norm_ref[...]) * _silu(a_ref[:, 3 * bw:4 * bw]))


def _deltanet(ga, alog, dtb, norm, seq, tb=512):
    t = ga.shape[0]
    return pl.pallas_call(
        functools.partial(_deltanet_kernel, tb=tb, blocks_per_seq=seq // tb),
        grid=(t // tb,),
        in_specs=[pl.BlockSpec((tb, A_WIDTH), lambda i: (i, 0)),
                  _const_spec((1, BRANCH_WIDTH)), _const_spec((1, BRANCH_WIDTH)),
                  _const_spec((1, BRANCH_WIDTH))],
        out_specs=pl.BlockSpec((tb, BRANCH_WIDTH), lambda i: (i, 0)),
        out_shape=jax.ShapeDtypeStruct((t, BRANCH_WIDTH), BF16),
        scratch_shapes=[pltpu.VMEM((BRANCH_WIDTH, BRANCH_WIDTH), F32)],
        compiler_params=_params("arbitrary"),
        name="deltanet",
    )(ga, alog, dtb, norm)


def _gla_kernel(c_ref, norm_ref, o_ref, state_ref, *, tb, blocks_per_seq):
    i = pl.program_id(0)

    @pl.when(i % blocks_per_seq == 0)
    def _():
        state_ref[...] = jnp.zeros_like(state_ref)

    qw = HEADS * GLA_DK
    bw = BRANCH_WIDTH
    nchunk = tb // CHUNK
    la_off = 2 * qw + 2 * bw
    kshift = GLA_DK.bit_length() - 1
    vshift = HEAD_DIM.bit_length() - 1
    r64 = lax.broadcasted_iota(jnp.int32, (CHUNK, CHUNK), 0)
    c64 = lax.broadcasted_iota(jnp.int32, (CHUNK, CHUNK), 1)
    lmat = _bf(jnp.where(c64 <= r64, 1.0, 0.0))
    t_idx = lax.broadcasted_iota(jnp.int32, (CHUNK, bw), 0)
    lane_v = lax.broadcasted_iota(jnp.int32, (CHUNK, bw), 1)
    lane_k = lax.broadcasted_iota(jnp.int32, (CHUNK, qw), 1)
    lower = (lane_v & (HEAD_DIM - 1)) <= t_idx
    sel_v = [_bf(jnp.where((lane_v >> vshift) == h, 1.0, 0.0)) for h in range(HEADS)]
    sel_k = [_bf(jnp.where((lane_k >> kshift) == h, 1.0, 0.0)) for h in range(HEADS)]
    same_head = (lax.broadcasted_iota(jnp.int32, (bw, qw), 0) >> vshift) == \
                (lax.broadcasted_iota(jnp.int32, (bw, qw), 1) >> kshift)

    def bd(y, sel):
        return jnp.concatenate([y * m for m in sel], axis=0)

    chunks = range(nchunk)
    rows = lambda col0, width, c: c_ref[c * CHUNK:(c + 1) * CHUNK, col0:col0 + width]

    bcums = []
    for c in chunks:
        a1, a2 = _split(rows(la_off, qw, c), 2)
        bcums.append(_dot(lmat, a1) + _dot(lmat, a2))
    q_ins = [_bf(rows(0, qw, c) * (GLA_DK ** -0.5) * jnp.exp(b)) for c, b in zip(chunks, bcums)]
    k_ins = [_bf(rows(qw, qw, c) * jnp.exp(-b)) for c, b in zip(chunks, bcums)]
    vbs = [_bf(rows(2 * qw, bw, c)) for c in chunks]
    attns = [_bf(jnp.where(lower, _dot_nt(qi, bd(ki, sel_k)), 0.0)) for qi, ki in zip(q_ins, k_ins)]
    intras = [_dot(a, bd(vb, sel_v)) for a, vb in zip(attns, vbs)]
    lasts = [b[CHUNK - 1:CHUNK, :] for b in bcums]
    k_tails = [_bf(rows(qw, qw, c) * jnp.exp(l - b)) for c, l, b in zip(chunks, lasts, bcums)]
    cds = [jnp.exp(l) for l in lasts]

    st = state_ref[...]
    o_rows = []
    for c in chunks:
        o_rows.append(intras[c] + _dot_nt(q_ins[c], _bf(st)))
        st = st * cds[c] + jnp.where(same_head, _dot_tn(vbs[c], k_tails[c]), 0.0)
    state_ref[...] = st
    o_all = jnp.concatenate(o_rows, axis=0)
    gate = c_ref[:, 2 * qw + bw:2 * qw + 2 * bw]
    o_ref[...] = _bf(_head_rms(o_all, _head_ones(), norm_ref[...]) * _silu(gate))


def _gla(gc, norm, seq, tb=512):
    t = gc.shape[0]
    return pl.pallas_call(
        functools.partial(_gla_kernel, tb=tb, blocks_per_seq=seq // tb),
        grid=(t // tb,),
        in_specs=[pl.BlockSpec((tb, C_WIDTH), lambda i: (i, 0)), _const_spec((1, BRANCH_WIDTH))],
        out_specs=pl.BlockSpec((tb, BRANCH_WIDTH), lambda i: (i, 0)),
        out_shape=jax.ShapeDtypeStruct((t, BRANCH_WIDTH), BF16),
        scratch_shapes=[pltpu.VMEM((BRANCH_WIDTH, HEADS * GLA_DK), F32)],
        compiler_params=_params("arbitrary"),
        name="gla",
    )(gc, norm)


VT_ROWS = HEAD_DIM + 16


def _store_vt(vt_ref, v):
    vt = v.T
    tokens = v.shape[0]
    pad_rows = VT_ROWS - HEAD_DIM
    ones_row = _bf(jnp.where(lax.broadcasted_iota(jnp.int32, (pad_rows, tokens), 0) == 0, 1.0, 0.0))
    for h in range(HEADS):
        vt_ref[0, h * VT_ROWS:h * VT_ROWS + HEAD_DIM, :] = _bf(vt[h * HEAD_DIM:(h + 1) * HEAD_DIM, :])
        vt_ref[0, h * VT_ROWS + HEAD_DIM:(h + 1) * VT_ROWS, :] = ones_row


def _mla_operands(pb, qn_ref, kvn_ref, wq_ref, wkv_ref, cos_ref, sin_ref, q_ref, k_ref, vt_ref):
    cq = pb[:, 0:MLA_Q_LORA]
    ckv = pb[:, MLA_Q_LORA:MLA_Q_LORA + MLA_KV_LORA]
    off = MLA_Q_LORA + MLA_KV_LORA
    kpe = pb[:, off:off + SLOT]
    kpe_rot = pb[:, off + SLOT:off + 2 * SLOT]
    cos = cos_ref[...]
    sin = sin_ref[...]

    qa = _dot(_bf(_rms(cq, qn_ref[...])), wq_ref[...])
    kv = _dot(_bf(_rms(ckv, kvn_ref[...])), wkv_ref[...])
    k_pe = kpe * cos + kpe_rot * sin
    nslot = HEADS * SLOT
    for h in range(HEADS):
        sl = slice(h * SLOT, (h + 1) * SLOT)
        q_rope = qa[:, sl] * cos + qa[:, nslot + h * SLOT:nslot + (h + 1) * SLOT] * sin
        q_ref[:, sl] = _bf(q_rope * ((MLA_NOPE + MLA_ROPE) ** -0.5 * LOG2E))
        k_ref[:, sl] = _bf(kv[:, sl] + k_pe)
    _store_vt(vt_ref, kv[:, nslot:nslot + BRANCH_WIDTH])


def _fox_operands(pd, fb_ref, carry_ref, q_ref, k_ref, vt_ref):
    tb = pd.shape[0]
    bw = BRANCH_WIDTH
    log_f = _log_sigmoid(pd[:, 3 * bw:3 * bw + LANE] + fb_ref[...])
    lmat = _bf(jnp.where(_tri(tb), 1.0, 0.0))
    fcum = _dot_ones(lmat, log_f) + carry_ref[...]
    carry_ref[...] = fcum[tb - 1:tb, :]
    f1, f2, f3 = [f.astype(F32) for f in _split(fcum * LOG2E, 3)]

    lane = lax.broadcasted_iota(jnp.int32, (tb, LANE), 1)
    scale = HEAD_DIM ** -0.5 * LOG2E
    for h in range(HEADS):
        pair, odd = divmod(h, 2)
        qp = pd[:, pair * LANE:(pair + 1) * LANE] * scale
        kp = pd[:, bw + pair * LANE:bw + (pair + 1) * LANE]
        head_lanes = (lane >= HEAD_DIM) if odd else (lane < HEAD_DIM)
        a0 = 0 if odd else HEAD_DIM
        fb = [jnp.broadcast_to(f[:, h:h + 1], (tb, LANE)) for f in (f1, f2, f3)]
        qx = jnp.zeros((tb, LANE), F32)
        kx = jnp.zeros((tb, LANE), F32)
        for j in range(3):
            qx = jnp.where(lane == a0 + j, fb[j], qx)
            qx = jnp.where(lane == a0 + 3 + j, 1.0, qx)
            kx = jnp.where(lane == a0 + j, 1.0, kx)
            kx = jnp.where(lane == a0 + 3 + j, -fb[j], kx)
        q_ref[:, h * SLOT:(h + 1) * SLOT] = _bf(jnp.where(head_lanes, qp, qx))
        k_ref[:, h * SLOT:(h + 1) * SLOT] = _bf(jnp.where(head_lanes, kp, kx))
    _store_vt(vt_ref, pd[:, 2 * bw:3 * bw])


def _flash_kernel(q_ref, k_ref, vt_ref, o_ref, m_ref, acc_ref, s_ref, *, chunk_mask, tq, tk):
    assert tq == tk
    qi = pl.program_id(1)
    m_ref[...] = jnp.full_like(m_ref, -jnp.inf)
    acc_ref[...] = jnp.zeros_like(acc_ref)

    def scores(ki, slot, hh):
        s_ref[slot, hh] = _dot_nt(k_ref[ki, :, hh * SLOT:(hh + 1) * SLOT],
                                  q_ref[:, hh * SLOT:(hh + 1) * SLOT])

    def softmax_pv(ki, slot, hh, masked):
        s = s_ref[slot, hh]
        if masked:
            kpos = lax.broadcasted_iota(jnp.int32, s.shape, 0)
            qpos = lax.broadcasted_iota(jnp.int32, s.shape, 1)
            if chunk_mask:
                shift = CHUNK.bit_length() - 1
                ok = (kpos >> shift) <= (qpos >> shift)
            else:
                ok = kpos <= qpos
            s = jnp.where(ok, s, -jnp.inf)
        m_prev = m_ref[hh]
        m_new = jnp.maximum(m_prev, jnp.max(s, axis=0, keepdims=True))
        p = _bf(jnp.exp2(s - m_new))
        pv = _dot(vt_ref[ki, hh * VT_ROWS:(hh + 1) * VT_ROWS, :], p)
        acc_ref[hh] = jnp.exp2(m_prev - m_new) * acc_ref[hh] + pv
        m_ref[hh] = m_new

    def step(ki, slot, masked=False, prefetch=True):
        for hh in range(HEADS):
            if prefetch:
                scores(ki + 1, 1 - slot, hh)
            softmax_pv(ki, slot, hh, masked)

    for hh in range(HEADS):
        scores(0, 0, hh)

    def pair(j, carry):
        step(2 * j, 0)
        step(2 * j + 1, 1)
        return carry

    lax.fori_loop(0, qi // 2, pair, 0)

    @pl.when(qi % 2 == 0)
    def _():
        step(qi, 0, masked=True, prefetch=False)

    @pl.when(qi % 2 == 1)
    def _():
        step(qi - 1, 0)
        step(qi, 1, masked=True, prefetch=False)

    ot = jnp.concatenate([acc_ref[hh][0:HEAD_DIM] / acc_ref[hh][HEAD_DIM:HEAD_DIM + 1]
                          for hh in range(HEADS)], axis=0)
    o_ref[...] = _bf(ot.T)


def _flash(q, k, vt3, batch, seq, chunk_mask, tq):
    tk = tq
    nq = seq // tq
    nk = seq // tk
    t = q.shape[0]
    k3 = k.reshape(t // tk, tk, HEADS * SLOT)
    return pl.pallas_call(
        functools.partial(_flash_kernel, chunk_mask=chunk_mask, tq=tq, tk=tk),
        grid=(batch, nq),
        in_specs=[pl.BlockSpec((tq, HEADS * SLOT), lambda b, qi: (b * nq + qi, 0)),
                  pl.BlockSpec((nk, tk, HEADS * SLOT), lambda b, qi: (b, 0, 0)),
                  pl.BlockSpec((nk, HEADS * VT_ROWS, tk), lambda b, qi: (b, 0, 0))],
        out_specs=pl.BlockSpec((tq, BRANCH_WIDTH), lambda b, qi: (b * nq + qi, 0)),
        out_shape=jax.ShapeDtypeStruct((t, BRANCH_WIDTH), BF16),
        scratch_shapes=[pltpu.VMEM((HEADS, 1, tq), F32), pltpu.VMEM((HEADS, VT_ROWS, tq), F32),
                        pltpu.VMEM((2, HEADS, tk, tq), F32)],
        compiler_params=_params("parallel", "arbitrary"),
        name="flash_chunk" if chunk_mask else "flash_causal",
    )(q, k3, vt3)


def _merge_kernel(x_ref, oa_ref, ob_ref, oc_ref, od_ref, gpre_ref, wg_ref, bg_ref, wup_ref,
                  wout_ref, gpost_ref, o_ref):
    x = x_ref[...]
    h = _bf(_rms(x, gpre_ref[...]))
    merged = jnp.zeros(x.shape, F32)
    for i, br in enumerate((oa_ref, ob_ref, oc_ref, od_ref)):
        gate = jax.nn.sigmoid(_dot(h, wg_ref[i]) + bg_ref[i:i + 1, :])
        merged = merged + gate * _dot(br[...], wup_ref[i])
    y = _dot(_bf(merged), wout_ref[...])
    o_ref[...] = x + _rms(y, gpost_ref[...])


def _merge(x2, oa, ob, oc, od, gpre, wg, bg, wup, wout, gpost, tm=512):
    t = x2.shape[0]
    row = lambda w: pl.BlockSpec((tm, w), lambda i: (i, 0))
    return pl.pallas_call(
        _merge_kernel,
        grid=(t // tm,),
        in_specs=[row(D_MODEL)] + [row(BRANCH_WIDTH)] * 4 +
                 [_const_spec((1, D_MODEL)), _const_spec((4, D_MODEL, D_MODEL)),
                  _const_spec((4, D_MODEL)), _const_spec((4, BRANCH_WIDTH, D_MODEL)),
                  _const_spec((D_MODEL, D_MODEL)), _const_spec((1, D_MODEL))],
        out_specs=row(D_MODEL),
        out_shape=jax.ShapeDtypeStruct((t, D_MODEL), F32),
        compiler_params=_params("parallel"),
        name="merge",
    )(x2, oa, ob, oc, od, gpre, wg, bg, wup, wout, gpost)


def _mlp_kernel(x_ref, gpre_ref, w1_ref, w2_ref, gpost_ref, o_ref, *, ff_chunk):
    x = x_ref[...]
    h = _bf(_rms(x, gpre_ref[...]))
    y = jnp.zeros(x.shape, F32)
    for c in range(D_FF // ff_chunk):
        u = jnp.maximum(_dot(h, w1_ref[:, c * ff_chunk:(c + 1) * ff_chunk]), 0.0)
        y = y + _dot(_bf(u * u), w2_ref[c * ff_chunk:(c + 1) * ff_chunk, :])
    o_ref[...] = x + _rms(y, gpost_ref[...])


def _mlp(x2, gpre, w1, w2, gpost, tm=512, ff_chunk=1024):
    t = x2.shape[0]
    row = pl.BlockSpec((tm, D_MODEL), lambda i: (i, 0))
    return pl.pallas_call(
        functools.partial(_mlp_kernel, ff_chunk=ff_chunk),
        grid=(t // tm,),
        in_specs=[row, _const_spec((1, D_MODEL)), _const_spec((D_MODEL, D_FF)),
                  _const_spec((D_FF, D_MODEL)), _const_spec((1, D_MODEL))],
        out_specs=row,
        out_shape=jax.ShapeDtypeStruct((t, D_MODEL), F32),
        compiler_params=_params("parallel"),
        name="mlp",
    )(x2, gpre, w1, w2, gpost)


_IN_SIZES = (
    HEADS * 3 * HEAD_DIM, HEADS * HEAD_DIM, HEADS, HEADS,
    MLA_Q_LORA, MLA_KV_LORA, MLA_ROPE,
    HEADS * GLA_DK, HEADS * GLA_DK, HEADS * HEAD_DIM, HEADS * HEAD_DIM, GLA_RANK,
    HEADS * HEAD_DIM, HEADS * HEAD_DIM, HEADS * HEAD_DIM, HEADS,
)


def _rot_cols(w):
    half = w.shape[-1] // 2
    return jnp.concatenate([-w[..., half:], w[..., :half]], axis=-1)


def _pad_cols(w, width):
    return jnp.pad(w, ((0, 0), (0, width - w.shape[-1])))


def _layout_in_weight(w_in):
    splits = np.cumsum(_IN_SIZES)[:-1].tolist()
    (dn_qkv, dn_z, dn_b, dn_a, cq, ckv, kpe, gq, gk, gv, gg, glr, fq, fk, fv, ff) = jnp.split(w_in, splits, axis=1)
    zeros = lambda n: jnp.zeros((w_in.shape[0], n), w_in.dtype)
    pe_slot = lambda w: jnp.concatenate([zeros(MLA_NOPE), w, zeros(SLOT - MLA_NOPE - MLA_ROPE)], axis=1)
    cols = [dn_qkv, dn_z, jnp.repeat(dn_a, HEAD_DIM, axis=1), jnp.repeat(dn_b, HEAD_DIM, axis=1),
            cq, ckv, pe_slot(kpe), pe_slot(_rot_cols(kpe)),
            gq, gk, gv, gg, _pad_cols(glr, LANE),
            fq, fk, fv, _pad_cols(ff, LANE)]
    w = jnp.concatenate([_bf(c) for c in cols], axis=1)
    assert w.shape[1] == IN_TOTAL
    return w


def _layout_q_up(w):
    zeros = lambda n: jnp.zeros((w.shape[0], n), w.dtype)
    per = MLA_NOPE + MLA_ROPE
    plain, rot = [], []
    for h in range(HEADS):
        nope = w[:, h * per:h * per + MLA_NOPE]
        pe = w[:, h * per + MLA_NOPE:(h + 1) * per]
        plain += [nope, pe, zeros(SLOT - per)]
        rot += [zeros(MLA_NOPE), _rot_cols(pe), zeros(SLOT - per)]
    return _bf(jnp.concatenate(plain + rot, axis=1))


def _layout_kv_up(w):
    zeros = lambda n: jnp.zeros((w.shape[0], n), w.dtype)
    per = MLA_NOPE + HEAD_DIM
    ks, vs = [], []
    for h in range(HEADS):
        ks += [w[:, h * per:h * per + MLA_NOPE], zeros(SLOT - MLA_NOPE)]
        vs += [w[:, h * per + MLA_NOPE:(h + 1) * per]]
    return _bf(jnp.concatenate(ks + vs, axis=1))


def _rope_tables(seq):
    inv_freq = 1.0 / (ROPE_THETA ** (jnp.arange(0, MLA_ROPE, 2, dtype=F32) / MLA_ROPE))
    ang = jnp.arange(seq, dtype=F32)[:, None] * inv_freq[None, :]
    ang = jnp.concatenate([ang, ang], axis=-1)
    pad = jnp.zeros((seq, SLOT - MLA_NOPE - MLA_ROPE), F32)
    cos_t = jnp.concatenate([jnp.ones((seq, MLA_NOPE), F32), jnp.cos(ang), pad], axis=1)
    sin_t = jnp.concatenate([jnp.zeros((seq, MLA_NOPE), F32), jnp.sin(ang), pad], axis=1)
    return cos_t, sin_t


TOKEN_TILE = 512


def _layer(x2, batch, seq, p):
    ga, gc, q_b, k_b, vt_b, q_d, k_d, vt_d = _inproj(x2, p, seq, TOKEN_TILE)
    o_a = _deltanet(ga, p["alog"], p["dtb"], p["dn_norm"], seq)
    o_b = _flash(q_b, k_b, vt_b, batch, seq, True, TOKEN_TILE)
    o_c = _gla(gc, p["gla_norm"], seq)
    o_d = _flash(q_d, k_d, vt_d, batch, seq, False, TOKEN_TILE)
    x2 = _merge(x2, o_a, o_b, o_c, o_d, p["norm_mix_pre"], p["w_gate"], p["b_gate"],
                p["w_branch_up"], p["w_out"], p["norm_mix_post"])
    return _mlp(x2, p["norm_mlp_pre"], p["w_mlp_in"], p["w_mlp_out"], p["norm_mlp_post"])


def kernel(x, norm_mix_pre, norm_mix_post, norm_mlp_pre, norm_mlp_post, w_in, dn_conv, dn_a_log, dn_dt_bias, dn_norm, mla_q_norm, mla_w_q_up, mla_kv_norm, mla_w_kv_up, gla_w_gate_up, gla_gate_bias, gla_norm, fox_f_bias, w_branch_up, w_gate, b_gate, w_out, w_mlp_in, w_mlp_out):
    batch, seq, d = x.shape
    depth = w_in.shape[0]
    cos_t, sin_t = _rope_tables(seq)
    x2 = x.reshape(batch * seq, d)
    row = lambda a: a.reshape(1, -1)
    for l in range(depth):
        p = {
            "norm_mix_pre": row(norm_mix_pre[l]), "norm_mix_post": row(norm_mix_post[l]),
            "norm_mlp_pre": row(norm_mlp_pre[l]), "norm_mlp_post": row(norm_mlp_post[l]),
            "w_cat": _layout_in_weight(w_in[l]),
            "dn_conv": dn_conv[l],
            "wgu": _bf(jnp.pad(gla_w_gate_up[l], ((0, LANE - GLA_RANK), (0, 0)))),
            "gbias": row(gla_gate_bias[l]),
            "alog": row(jnp.repeat(dn_a_log[l], HEAD_DIM)), "dtb": row(jnp.repeat(dn_dt_bias[l], HEAD_DIM)),
            "dn_norm": row(jnp.tile(dn_norm[l], HEADS)),
            "mla_q_norm": row(mla_q_norm[l]), "mla_kv_norm": row(mla_kv_norm[l]),
            "wq": _layout_q_up(mla_w_q_up[l]), "wkv": _layout_kv_up(mla_w_kv_up[l]),
            "cos": cos_t, "sin": sin_t,
            "gla_norm": row(jnp.tile(gla_norm[l], HEADS)),
            "fbias": row(jnp.pad(fox_f_bias[l], (0, LANE - HEADS))),
            "w_gate": _bf(w_gate[l]), "b_gate": b_gate[l], "w_branch_up": _bf(w_branch_up[l]),
            "w_out": _bf(w_out[l]), "w_mlp_in": _bf(w_mlp_in[l]), "w_mlp_out": _bf(w_mlp_out[l]),
        }
        x2 = _layer(x2, batch, seq, p)
    return x2.reshape(batch, seq, d)
```

```python
import functools
import math

import jax
import jax.numpy as jnp
import numpy as np
from jax import lax
from jax.experimental import pallas as pl
from jax.experimental.pallas import tpu as pltpu

F32 = jnp.float32
BF16 = jnp.bfloat16

D_MODEL = 1024
CHUNK = 64
EPS = 1e-6
HEADS = 4
HEAD_DIM = 64
BRANCH_WIDTH = HEADS * HEAD_DIM
D_FF = 4 * D_MODEL
CONV_K = 4
MLA_NOPE = HEAD_DIM
MLA_ROPE = HEAD_DIM // 2
MLA_Q_LORA = D_MODEL // 4
MLA_KV_LORA = D_MODEL // 8
ROPE_THETA = 10000.0
GLA_DK = HEAD_DIM // 2
GLA_RANK = 16
GLA_TAU = 16.0
LOG2E = math.log2(math.e)
LANE = 128
SLOT = LANE

A_WIDTH = 3 * BRANCH_WIDTH + 3 * BRANCH_WIDTH
B_WIDTH = MLA_Q_LORA + MLA_KV_LORA + 2 * SLOT
C_WIDTH = 2 * HEADS * GLA_DK + 2 * BRANCH_WIDTH + LANE
D_WIDTH = 3 * BRANCH_WIDTH + LANE
IN_TOTAL = A_WIDTH + B_WIDTH + C_WIDTH + D_WIDTH

VMEM_LIMIT = 56 * 1024 * 1024


def _dot(a, b):
    return jnp.dot(a, b, preferred_element_type=F32)


def _dot_nt(a, b):
    return lax.dot_general(a, b, (((1,), (1,)), ((), ())), preferred_element_type=F32)


def _dot_tn(a, b):
    return lax.dot_general(a, b, (((0,), (0,)), ((), ())), preferred_element_type=F32)


def _bf(x):
    return x.astype(BF16)


def _split(x, pieces):
    out = []
    for _ in range(pieces - 1):
        hi = _bf(x)
        out.append(hi)
        x = x - hi.astype(F32)
    out.append(_bf(x))
    return out


def _dot3(a, b):
    ah, al = _split(a, 2)
    bh, bl = _split(b, 2)
    return _dot(ah, bh) + (_dot(ah, bl) + _dot(al, bh))


def _dot_ones(ones_bf, x):
    x1, x2, x3 = _split(x, 3)
    return _dot(ones_bf, x1) + (_dot(ones_bf, x2) + _dot(ones_bf, x3))


def _rms(x, gain):
    return x * lax.rsqrt(jnp.mean(x * x, axis=-1, keepdims=True) + EPS) * gain


def _silu(x):
    return x * jax.nn.sigmoid(x)


def _softplus(x):
    return jnp.maximum(x, 0.0) + jnp.log1p(jnp.exp(-jnp.abs(x)))


def _log_sigmoid(x):
    return jnp.minimum(x, 0.0) - jnp.log1p(jnp.exp(-jnp.abs(x)))


def _tri(n, strict=False):
    r = lax.broadcasted_iota(jnp.int32, (n, n), 0)
    c = lax.broadcasted_iota(jnp.int32, (n, n), 1)
    return (c < r) if strict else (c <= r)


def _params(*sem):
    return pltpu.CompilerParams(dimension_semantics=sem, vmem_limit_bytes=VMEM_LIMIT)


def _const_spec(shape):
    nd = len(shape)
    return pl.BlockSpec(shape, lambda *_: (0,) * nd, pipeline_mode=pl.Buffered(1))


def _inproj_kernel(x_ref, gain_ref, w_ref, convw_ref, wgu_ref, gbias_ref,
                   qn_ref, kvn_ref, wq_ref, wkv_ref, cos_ref, sin_ref, fb_ref,
                   oa_ref, oc_ref, qb_ref, kb_ref, vtb_ref, qd_ref, kd_ref, vtd_ref,
                   xe_ref, carry_ref, *, tm, tiles_per_seq):
    i = pl.program_id(0)
    h = _bf(_rms(x_ref[...], gain_ref[...]))

    qkv_w = 3 * BRANCH_WIDTH
    cw = BRANCH_WIDTH

    @pl.when(i % tiles_per_seq == 0)
    def _():
        xe_ref[0:8, :] = jnp.zeros((8, qkv_w), F32)
        carry_ref[...] = jnp.zeros_like(carry_ref)

    def conv_cols(c0):
        y = jnp.zeros((tm, cw), F32)
        for j in range(CONV_K):
            y = y + convw_ref[j:j + 1, c0:c0 + cw] * xe_ref[pl.ds(8 - (CONV_K - 1) + j, tm), c0:c0 + cw]
        oa_ref[:, c0:c0 + cw] = _silu(y)
        xe_ref[0:8, c0:c0 + cw] = xe_ref[tm:tm + 8, c0:c0 + cw]

    off_c = A_WIDTH + B_WIDTH
    off_d = off_c + C_WIDTH
    lr_off = C_WIDTH - LANE
    pd = _dot(h, w_ref[:, off_d:off_d + D_WIDTH])
    pb = _dot(h, w_ref[:, A_WIDTH:A_WIDTH + B_WIDTH])
    fox = _fox_stages(pd, fb_ref, carry_ref, qd_ref, kd_ref, vtd_ref)
    mla = _mla_stages(pb, qn_ref, kvn_ref, wq_ref, wkv_ref, cos_ref, sin_ref, qb_ref, kb_ref, vtb_ref)

    xe_ref[8:8 + tm, 0:2 * cw] = _dot(h, w_ref[:, 0:2 * cw])
    for stage in fox:
        stage()
    pa = _dot(h, w_ref[:, 2 * cw:4 * cw])
    xe_ref[8:8 + tm, 2 * cw:3 * cw] = pa[:, 0:cw]
    oa_ref[:, 3 * cw:4 * cw] = pa[:, cw:2 * cw]
    conv_cols(0)
    conv_cols(cw)
    oa_ref[:, 4 * cw:A_WIDTH] = _dot(h, w_ref[:, 4 * cw:A_WIDTH])
    conv_cols(2 * cw)
    oc_ref[:, 0:lr_off] = _dot(h, w_ref[:, off_c:off_c + lr_off])
    for stage in mla:
        stage()
    lr = _dot(h, w_ref[:, off_c + lr_off:off_c + C_WIDTH])
    gate = _dot(_bf(lr), wgu_ref[...]) + gbias_ref[...]
    oc_ref[:, lr_off:C_WIDTH] = _log_sigmoid(gate) * (1.0 / GLA_TAU)


def _inproj(x2, p, seq, tm):
    t = x2.shape[0]
    tps = seq // tm
    nslot = HEADS * SLOT
    row = lambda w: pl.BlockSpec((tm, w), lambda i: (i, 0))
    pos = pl.BlockSpec((tm, SLOT), lambda i: (i % tps, 0))
    vt_spec = pl.BlockSpec((1, HEADS * VT_ROWS, tm), lambda i: (i, 0, 0))
    slots = jax.ShapeDtypeStruct((t, nslot), BF16)
    vts = jax.ShapeDtypeStruct((t // tm, HEADS * VT_ROWS, tm), BF16)
    return pl.pallas_call(
        functools.partial(_inproj_kernel, tm=tm, tiles_per_seq=tps),
        grid=(t // tm,),
        in_specs=[row(D_MODEL), _const_spec((1, D_MODEL)), _const_spec((D_MODEL, IN_TOTAL)),
                  _const_spec((CONV_K, 3 * BRANCH_WIDTH)), _const_spec((LANE, LANE)),
                  _const_spec((1, LANE)),
                  _const_spec((1, MLA_Q_LORA)), _const_spec((1, MLA_KV_LORA)),
                  _const_spec((MLA_Q_LORA, 2 * nslot)), _const_spec((MLA_KV_LORA, nslot + BRANCH_WIDTH)),
                  pos, pos, _const_spec((1, LANE))],
        out_specs=[row(A_WIDTH), row(C_WIDTH), row(nslot), row(nslot), vt_spec,
                   row(nslot), row(nslot), vt_spec],
        out_shape=[jax.ShapeDtypeStruct((t, A_WIDTH), F32), jax.ShapeDtypeStruct((t, C_WIDTH), F32),
                   slots, slots, vts, slots, slots, vts],
        scratch_shapes=[pltpu.VMEM((tm + 8, 3 * BRANCH_WIDTH), F32), pltpu.VMEM((1, LANE), F32)],
        compiler_params=_params("arbitrary"),
        name="inproj",
    )(x2, p["norm_mix_pre"], p["w_cat"], p["dn_conv"], p["wgu"], p["gbias"],
      p["mla_q_norm"], p["mla_kv_norm"], p["wq"], p["wkv"], p["cos"], p["sin"], p["fbias"])


def _head_ones():
    r = lax.broadcasted_iota(jnp.int32, (BRANCH_WIDTH, BRANCH_WIDTH), 0)
    c = lax.broadcasted_iota(jnp.int32, (BRANCH_WIDTH, BRANCH_WIDTH), 1)
    shift = HEAD_DIM.bit_length() - 1
    return _bf(jnp.where((r >> shift) == (c >> shift), 1.0, 0.0))


def _head_sums(x, ones_bd):
    x1, x2 = _split(x, 2)
    return _dot(x1, ones_bd) + _dot(x2, ones_bd)


def _head_rms(x, ones_bd, gain):
    return x * lax.rsqrt(_head_sums(x * x, ones_bd) * (1.0 / HEAD_DIM) + EPS) * gain


def _deltanet_kernel(a_ref, alog_ref, dtb_ref, norm_ref, o_ref, state_ref, *, tb, blocks_per_seq):
    i = pl.program_id(0)

    @pl.when(i % blocks_per_seq == 0)
    def _():
        state_ref[...] = jnp.zeros_like(state_ref)

    bw = BRANCH_WIDTH
    nchunk = tb // CHUNK
    hshift = HEAD_DIM.bit_length() - 1
    t_idx = lax.broadcasted_iota(jnp.int32, (CHUNK, bw), 0)
    lane = lax.broadcasted_iota(jnp.int32, (CHUNK, bw), 1)
    s_idx = lane & (HEAD_DIM - 1)
    lower = s_idx <= t_idx
    strict = s_idx < t_idx
    same_block = lambda log2s: (t_idx >> log2s) == (s_idx >> log2s)
    eye = jnp.where(s_idx == t_idx, 1.0, 0.0).astype(F32)
    strict_f = jnp.where(strict, 1.0, 0.0).astype(F32)
    head_sel = [_bf(jnp.where((lane >> hshift) == h, 1.0, 0.0)) for h in range(HEADS)]
    r64 = lax.broadcasted_iota(jnp.int32, (CHUNK, CHUNK), 0)
    c64 = lax.broadcasted_iota(jnp.int32, (CHUNK, CHUNK), 1)
    lmat = _bf(jnp.where(c64 <= r64, 1.0, 0.0))
    r256 = lax.broadcasted_iota(jnp.int32, (bw, bw), 0)
    c256 = lax.broadcasted_iota(jnp.int32, (bw, bw), 1)
    same_head = (r256 >> hshift) == (c256 >> hshift)
    ones_bd = _bf(jnp.where(same_head, 1.0, 0.0))

    def bd(y):
        return jnp.concatenate([y * m for m in head_sel], axis=0)

    def mm(a, b):
        return _dot(a, bd(b))

    def mm3(a, b):
        ah, al = _split(a, 2)
        bh, bl = _split(b, 2)
        bdh = bd(bh)
        return _dot(ah, bdh) + (_dot(ah, bd(bl)) + _dot(al, bdh))

    neg_a = -jnp.exp(alog_ref[...])
    g_all = neg_a * _softplus(a_ref[:, 4 * bw:5 * bw] + dtb_ref[...])
    beta_all = jax.nn.sigmoid(a_ref[:, 5 * bw:6 * bw])
    q_all = a_ref[:, 0:bw]
    k_all = a_ref[:, bw:2 * bw]
    q_all = q_all * lax.rsqrt(_head_sums(q_all * q_all, ones_bd) + EPS) * (HEAD_DIM ** -0.5)
    k_all = k_all * lax.rsqrt(_head_sums(k_all * k_all, ones_bd) + EPS)
    vb_all = a_ref[:, 2 * bw:3 * bw] * beta_all

    chunks = range(nchunk)
    rows = lambda x, c: x[c * CHUNK:(c + 1) * CHUNK, :]
    qs = [rows(q_all, c) for c in chunks]
    ks = [rows(k_all, c) for c in chunks]
    gs = [rows(g_all, c) for c in chunks]
    betas = [rows(beta_all, c) for c in chunks]

    cums = []
    for g in gs:
        g1, g2 = _split(jnp.concatenate([g * strict_f, g], axis=1), 2)
        cums.append(_dot(lmat, g1) + _dot(lmat, g2))
    diffs = [cm[:, 0:bw] for cm in cums]
    gcs = [cm[:, bw:2 * bw] for cm in cums]
    decays = [jnp.where(lower, jnp.exp(jnp.where(lower, d, 0.0)), 0.0) for d in diffs]
    kbs = [_bf(k) for k in ks]
    kqs = [_dot_nt(jnp.concatenate([kb, _bf(q)], axis=0), bd(kb)) for kb, q in zip(kbs, qs)]
    amats = [jnp.where(strict, b * kq[0:CHUNK] * d, 0.0) for b, kq, d in zip(betas, kqs, decays)]

    xs = [eye - jnp.where(same_block(1), a, 0.0) for a in amats]
    for log2s in range(1, CHUNK.bit_length() - 1):
        off = same_block(log2s + 1) & ~same_block(log2s)
        xbs = [_bf(x) for x in xs]
        xns = [_bf(mm(xb, _bf(jnp.where(off, a, 0.0)))) for xb, a in zip(xbs, amats)]
        xs = [x - mm(xn, xb) for x, xn, xb in zip(xs, xns, xbs)]
    resid = [eye - mm3(eye + a, x) for a, x in zip(amats, xs)]
    xs = [x + mm(_bf(x), _bf(r)) for x, r in zip(xs, resid)]

    egcs = [jnp.exp(gc) for gc in gcs]
    us = [mm3(x, rows(vb_all, c)) for x, c in zip(xs, chunks)]
    wbs = [_bf(mm(_bf(x), _bf(k * b * e))) for x, k, b, e in zip(xs, ks, betas, egcs)]
    qks = [_bf(kq[CHUNK:2 * CHUNK] * d) for kq, d in zip(kqs, decays)]
    q_decays = [_bf(q * e) for q, e in zip(qs, egcs)]
    lasts = [gc[CHUNK - 1:CHUNK, :] for gc in gcs]
    k_tails = [_bf(k * jnp.exp(l - gc)) for k, l, gc in zip(ks, lasts, gcs)]
    cds = [jnp.exp(l) for l in lasts]

    s = state_ref[...]
    o_rows = []
    for c in chunks:
        sb = _bf(s)
        vb = _bf(us[c] - _dot(wbs[c], sb))
        o_rows.append(_dot(q_decays[c], sb) + mm(qks[c], vb))
        s = s * cds[c] + jnp.where(same_head, _dot_tn(k_tails[c], vb), 0.0)
    state_ref[...] = s
    o_all = jnp.concatenate(o_rows, axis=0)
    o_ref[...] = _bf(_head_rms(o_all, ones_bd, norm_ref[...]) * _silu(a_ref[:, 3 * bw:4 * bw]))


def _deltanet(ga, alog, dtb, norm, seq, tb=512):
    t = ga.shape[0]
    return pl.pallas_call(
        functools.partial(_deltanet_kernel, tb=tb, blocks_per_seq=seq // tb),
        grid=(t // tb,),
        in_specs=[pl.BlockSpec((tb, A_WIDTH), lambda i: (i, 0)),
                  _const_spec((1, BRANCH_WIDTH)), _const_spec((1, BRANCH_WIDTH)),
                  _const_spec((1, BRANCH_WIDTH))],
        out_specs=pl.BlockSpec((tb, BRANCH_WIDTH), lambda i: (i, 0)),
        out_shape=jax.ShapeDtypeStruct((t, BRANCH_WIDTH), BF16),
        scratch_shapes=[pltpu.VMEM((BRANCH_WIDTH, BRANCH_WIDTH), F32)],
        compiler_params=_params("arbitrary"),
        name="deltanet",
    )(ga, alog, dtb, norm)


def _gla_kernel(c_ref, norm_ref, o_ref, state_ref, *, tb, blocks_per_seq):
    i = pl.program_id(0)

    @pl.when(i % blocks_per_seq == 0)
    def _():
        state_ref[...] = jnp.zeros_like(state_ref)

    qw = HEADS * GLA_DK
    bw = BRANCH_WIDTH
    nchunk = tb // CHUNK
    la_off = 2 * qw + 2 * bw
    kshift = GLA_DK.bit_length() - 1
    vshift = HEAD_DIM.bit_length() - 1
    r64 = lax.broadcasted_iota(jnp.int32, (CHUNK, CHUNK), 0)
    c64 = lax.broadcasted_iota(jnp.int32, (CHUNK, CHUNK), 1)
    lmat = _bf(jnp.where(c64 <= r64, 1.0, 0.0))
    t_idx = lax.broadcasted_iota(jnp.int32, (CHUNK, bw), 0)
    lane_v = lax.broadcasted_iota(jnp.int32, (CHUNK, bw), 1)
    lane_k = lax.broadcasted_iota(jnp.int32, (CHUNK, qw), 1)
    lower = (lane_v & (HEAD_DIM - 1)) <= t_idx
    sel_v = [_bf(jnp.where((lane_v >> vshift) == h, 1.0, 0.0)) for h in range(HEADS)]
    sel_k = [_bf(jnp.where((lane_k >> kshift) == h, 1.0, 0.0)) for h in range(HEADS)]
    same_head = (lax.broadcasted_iota(jnp.int32, (bw, qw), 0) >> vshift) == \
                (lax.broadcasted_iota(jnp.int32, (bw, qw), 1) >> kshift)

    def bd(y, sel):
        return jnp.concatenate([y * m for m in sel], axis=0)

    chunks = range(nchunk)
    rows = lambda col0, width, c: c_ref[c * CHUNK:(c + 1) * CHUNK, col0:col0 + width]

    bcums = []
    for c in chunks:
        a1, a2 = _split(rows(la_off, qw, c), 2)
        bcums.append(_dot(lmat, a1) + _dot(lmat, a2))
    q_ins = [_bf(rows(0, qw, c) * (GLA_DK ** -0.5) * jnp.exp(b)) for c, b in zip(chunks, bcums)]
    k_ins = [_bf(rows(qw, qw, c) * jnp.exp(-b)) for c, b in zip(chunks, bcums)]
    vbs = [_bf(rows(2 * qw, bw, c)) for c in chunks]
    attns = [_bf(jnp.where(lower, _dot_nt(qi, bd(ki, sel_k)), 0.0)) for qi, ki in zip(q_ins, k_ins)]
    intras = [_dot(a, bd(vb, sel_v)) for a, vb in zip(attns, vbs)]
    lasts = [b[CHUNK - 1:CHUNK, :] for b in bcums]
    k_tails = [_bf(rows(qw, qw, c) * jnp.exp(l - b)) for c, l, b in zip(chunks, lasts, bcums)]
    cds = [jnp.exp(l) for l in lasts]

    st = state_ref[...]
    o_rows = []
    for c in chunks:
        o_rows.append(intras[c] + _dot_nt(q_ins[c], _bf(st)))
        st = st * cds[c] + jnp.where(same_head, _dot_tn(vbs[c], k_tails[c]), 0.0)
    state_ref[...] = st
    o_all = jnp.concatenate(o_rows, axis=0)
    gate = c_ref[:, 2 * qw + bw:2 * qw + 2 * bw]
    o_ref[...] = _bf(_head_rms(o_all, _head_ones(), norm_ref[...]) * _silu(gate))


def _gla(gc, norm, seq, tb=512):
    t = gc.shape[0]
    return pl.pallas_call(
        functools.partial(_gla_kernel, tb=tb, blocks_per_seq=seq // tb),
        grid=(t // tb,),
        in_specs=[pl.BlockSpec((tb, C_WIDTH), lambda i: (i, 0)), _const_spec((1, BRANCH_WIDTH))],
        out_specs=pl.BlockSpec((tb, BRANCH_WIDTH), lambda i: (i, 0)),
        out_shape=jax.ShapeDtypeStruct((t, BRANCH_WIDTH), BF16),
        scratch_shapes=[pltpu.VMEM((BRANCH_WIDTH, HEADS * GLA_DK), F32)],
        compiler_params=_params("arbitrary"),
        name="gla",
    )(gc, norm)


VT_ROWS = HEAD_DIM + 16


def _store_vt(vt_ref, v):
    vt = v.T
    tokens = v.shape[0]
    pad_rows = VT_ROWS - HEAD_DIM
    ones_row = _bf(jnp.where(lax.broadcasted_iota(jnp.int32, (pad_rows, tokens), 0) == 0, 1.0, 0.0))
    for h in range(HEADS):
        vt_ref[0, h * VT_ROWS:h * VT_ROWS + HEAD_DIM, :] = _bf(vt[h * HEAD_DIM:(h + 1) * HEAD_DIM, :])
        vt_ref[0, h * VT_ROWS + HEAD_DIM:(h + 1) * VT_ROWS, :] = ones_row


def _mla_stages(pb, qn_ref, kvn_ref, wq_ref, wkv_ref, cos_ref, sin_ref, q_ref, k_ref, vt_ref):
    nslot = HEADS * SLOT
    st = {}

    def prologue():
        cq = pb[:, 0:MLA_Q_LORA]
        ckv = pb[:, MLA_Q_LORA:MLA_Q_LORA + MLA_KV_LORA]
        off = MLA_Q_LORA + MLA_KV_LORA
        st["cos"] = cos_ref[...]
        st["sin"] = sin_ref[...]
        st["qa"] = _dot(_bf(_rms(cq, qn_ref[...])), wq_ref[...])
        st["kv"] = _dot(_bf(_rms(ckv, kvn_ref[...])), wkv_ref[...])
        st["k_pe"] = pb[:, off:off + SLOT] * st["cos"] + pb[:, off + SLOT:off + 2 * SLOT] * st["sin"]

    def head(h):
        def run():
            sl = slice(h * SLOT, (h + 1) * SLOT)
            qa = st["qa"]
            q_rope = qa[:, sl] * st["cos"] + qa[:, nslot + h * SLOT:nslot + (h + 1) * SLOT] * st["sin"]
            q_ref[:, sl] = _bf(q_rope * ((MLA_NOPE + MLA_ROPE) ** -0.5 * LOG2E))
            k_ref[:, sl] = _bf(st["kv"][:, sl] + st["k_pe"])
        return run

    def store_v():
        _store_vt(vt_ref, st["kv"][:, nslot:nslot + BRANCH_WIDTH])

    return [prologue] + [head(h) for h in range(HEADS)] + [store_v]


def _fox_stages(pd, fb_ref, carry_ref, q_ref, k_ref, vt_ref):
    tb = pd.shape[0]
    bw = BRANCH_WIDTH
    scale = HEAD_DIM ** -0.5 * LOG2E
    st = {}

    def prologue():
        log_f = _log_sigmoid(pd[:, 3 * bw:3 * bw + LANE] + fb_ref[...])
        lmat = _bf(jnp.where(_tri(tb), 1.0, 0.0))
        fcum = _dot_ones(lmat, log_f) + carry_ref[...]
        carry_ref[...] = fcum[tb - 1:tb, :]
        st["f"] = [f.astype(F32) for f in _split(fcum * LOG2E, 3)]

    def head(h):
        def run():
            lane = lax.broadcasted_iota(jnp.int32, (tb, LANE), 1)
            pair, odd = divmod(h, 2)
            qp = pd[:, pair * LANE:(pair + 1) * LANE] * scale
            kp = pd[:, bw + pair * LANE:bw + (pair + 1) * LANE]
            head_lanes = (lane >= HEAD_DIM) if odd else (lane < HEAD_DIM)
            a0 = 0 if odd else HEAD_DIM
            fb = [jnp.broadcast_to(f[:, h:h + 1], (tb, LANE)) for f in st["f"]]
            qx = jnp.zeros((tb, LANE), F32)
            kx = jnp.zeros((tb, LANE), F32)
            for j in range(3):
                qx = jnp.where(lane == a0 + j, fb[j], qx)
                qx = jnp.where(lane == a0 + 3 + j, 1.0, qx)
                kx = jnp.where(lane == a0 + j, 1.0, kx)
                kx = jnp.where(lane == a0 + 3 + j, -fb[j], kx)
            q_ref[:, h * SLOT:(h + 1) * SLOT] = _bf(jnp.where(head_lanes, qp, qx))
            k_ref[:, h * SLOT:(h + 1) * SLOT] = _bf(jnp.where(head_lanes, kp, kx))
        return run

    def store_v():
        _store_vt(vt_ref, pd[:, 2 * bw:3 * bw])

    return [prologue] + [head(h) for h in range(HEADS)] + [store_v]


def _flash_kernel(q_ref, k_ref, vt_ref, o_ref, m_ref, acc_ref, s_ref, *, chunk_mask, tq, tk):
    assert tq == tk
    qi = pl.program_id(1)
    m_ref[...] = jnp.full_like(m_ref, -jnp.inf)
    acc_ref[...] = jnp.zeros_like(acc_ref)

    def scores(ki, slot, hh):
        s_ref[slot, hh] = _dot_nt(k_ref[ki, :, hh * SLOT:(hh + 1) * SLOT],
                                  q_ref[:, hh * SLOT:(hh + 1) * SLOT])

    def softmax_pv(ki, slot, hh, masked):
        s = s_ref[slot, hh]
        if masked:
            kpos = lax.broadcasted_iota(jnp.int32, s.shape, 0)
            qpos = lax.broadcasted_iota(jnp.int32, s.shape, 1)
            if chunk_mask:
                shift = CHUNK.bit_length() - 1
                ok = (kpos >> shift) <= (qpos >> shift)
            else:
                ok = kpos <= qpos
            s = jnp.where(ok, s, -jnp.inf)
        m_prev = m_ref[hh]
        m_new = jnp.maximum(m_prev, jnp.max(s, axis=0, keepdims=True))
        p = _bf(jnp.exp2(s - m_new))
        pv = _dot(vt_ref[ki, hh * VT_ROWS:(hh + 1) * VT_ROWS, :], p)
        acc_ref[hh] = jnp.exp2(m_prev - m_new) * acc_ref[hh] + pv
        m_ref[hh] = m_new

    def step(ki, slot, masked=False, prefetch=True):
        for hh in range(HEADS):
            if prefetch:
                scores(ki + 1, 1 - slot, hh)
            softmax_pv(ki, slot, hh, masked)

    for hh in range(HEADS):
        scores(0, 0, hh)

    def pair(j, carry):
        step(2 * j, 0)
        step(2 * j + 1, 1)
        return carry

    lax.fori_loop(0, qi // 2, pair, 0)

    @pl.when(qi % 2 == 0)
    def _():
        step(qi, 0, masked=True, prefetch=False)

    @pl.when(qi % 2 == 1)
    def _():
        step(qi - 1, 0)
        step(qi, 1, masked=True, prefetch=False)

    ot = jnp.concatenate([acc_ref[hh][0:HEAD_DIM] / acc_ref[hh][HEAD_DIM:HEAD_DIM + 1]
                          for hh in range(HEADS)], axis=0)
    o_ref[...] = _bf(ot.T)


def _flash(q, k, vt3, batch, seq, chunk_mask, tq):
    tk = tq
    nq = seq // tq
    nk = seq // tk
    t = q.shape[0]
    k3 = k.reshape(t // tk, tk, HEADS * SLOT)
    return pl.pallas_call(
        functools.partial(_flash_kernel, chunk_mask=chunk_mask, tq=tq, tk=tk),
        grid=(batch, nq),
        in_specs=[pl.BlockSpec((tq, HEADS * SLOT), lambda b, qi: (b * nq + qi, 0)),
                  pl.BlockSpec((nk, tk, HEADS * SLOT), lambda b, qi: (b, 0, 0)),
                  pl.BlockSpec((nk, HEADS * VT_ROWS, tk), lambda b, qi: (b, 0, 0))],
        out_specs=pl.BlockSpec((tq, BRANCH_WIDTH), lambda b, qi: (b * nq + qi, 0)),
        out_shape=jax.ShapeDtypeStruct((t, BRANCH_WIDTH), BF16),
        scratch_shapes=[pltpu.VMEM((HEADS, 1, tq), F32), pltpu.VMEM((HEADS, VT_ROWS, tq), F32),
                        pltpu.VMEM((2, HEADS, tk, tq), F32)],
        compiler_params=_params("parallel", "arbitrary"),
        name="flash_chunk" if chunk_mask else "flash_causal",
    )(q, k3, vt3)


def _merge_kernel(x_ref, oa_ref, ob_ref, oc_ref, od_ref, gpre_ref, wg_ref, bg_ref, wup_ref,
                  wout_ref, gpost_ref, o_ref):
    x = x_ref[...]
    h = _bf(_rms(x, gpre_ref[...]))
    merged = jnp.zeros(x.shape, F32)
    for i, br in enumerate((oa_ref, ob_ref, oc_ref, od_ref)):
        gate = jax.nn.sigmoid(_dot(h, wg_ref[i]) + bg_ref[i:i + 1, :])
        merged = merged + gate * _dot(br[...], wup_ref[i])
    y = _dot(_bf(merged), wout_ref[...])
    o_ref[...] = x + _rms(y, gpost_ref[...])


def _merge(x2, oa, ob, oc, od, gpre, wg, bg, wup, wout, gpost, tm=512):
    t = x2.shape[0]
    row = lambda w: pl.BlockSpec((tm, w), lambda i: (i, 0))
    return pl.pallas_call(
        _merge_kernel,
        grid=(t // tm,),
        in_specs=[row(D_MODEL)] + [row(BRANCH_WIDTH)] * 4 +
                 [_const_spec((1, D_MODEL)), _const_spec((4, D_MODEL, D_MODEL)),
                  _const_spec((4, D_MODEL)), _const_spec((4, BRANCH_WIDTH, D_MODEL)),
                  _const_spec((D_MODEL, D_MODEL)), _const_spec((1, D_MODEL))],
        out_specs=row(D_MODEL),
        out_shape=jax.ShapeDtypeStruct((t, D_MODEL), F32),
        compiler_params=_params("parallel"),
        name="merge",
    )(x2, oa, ob, oc, od, gpre, wg, bg, wup, wout, gpost)


def _mlp_kernel(x_ref, gpre_ref, w1_ref, w2_ref, gpost_ref, o_ref, *, ff_chunk):
    x = x_ref[...]
    h = _bf(_rms(x, gpre_ref[...]))
    y = jnp.zeros(x.shape, F32)
    for c in range(D_FF // ff_chunk):
        u = jnp.maximum(_dot(h, w1_ref[:, c * ff_chunk:(c + 1) * ff_chunk]), 0.0)
        y = y + _dot(_bf(u * u), w2_ref[c * ff_chunk:(c + 1) * ff_chunk, :])
    o_ref[...] = x + _rms(y, gpost_ref[...])


def _mlp(x2, gpre, w1, w2, gpost, tm=512, ff_chunk=1024):
    t = x2.shape[0]
    row = pl.BlockSpec((tm, D_MODEL), lambda i: (i, 0))
    return pl.pallas_call(
        functools.partial(_mlp_kernel, ff_chunk=ff_chunk),
        grid=(t // tm,),
        in_specs=[row, _const_spec((1, D_MODEL)), _const_spec((D_MODEL, D_FF)),
                  _const_spec((D_FF, D_MODEL)), _const_spec((1, D_MODEL))],
        out_specs=row,
        out_shape=jax.ShapeDtypeStruct((t, D_MODEL), F32),
        compiler_params=_params("parallel"),
        name="mlp",
    )(x2, gpre, w1, w2, gpost)


_IN_SIZES = (
    HEADS * 3 * HEAD_DIM, HEADS * HEAD_DIM, HEADS, HEADS,
    MLA_Q_LORA, MLA_KV_LORA, MLA_ROPE,
    HEADS * GLA_DK, HEADS * GLA_DK, HEADS * HEAD_DIM, HEADS * HEAD_DIM, GLA_RANK,
    HEADS * HEAD_DIM, HEADS * HEAD_DIM, HEADS * HEAD_DIM, HEADS,
)


def _rot_cols(w):
    half = w.shape[-1] // 2
    return jnp.concatenate([-w[..., half:], w[..., :half]], axis=-1)


def _pad_cols(w, width):
    return jnp.pad(w, ((0, 0), (0, width - w.shape[-1])))


def _layout_in_weight(w_in):
    splits = np.cumsum(_IN_SIZES)[:-1].tolist()
    (dn_qkv, dn_z, dn_b, dn_a, cq, ckv, kpe, gq, gk, gv, gg, glr, fq, fk, fv, ff) = jnp.split(w_in, splits, axis=1)
    zeros = lambda n: jnp.zeros((w_in.shape[0], n), w_in.dtype)
    pe_slot = lambda w: jnp.concatenate([zeros(MLA_NOPE), w, zeros(SLOT - MLA_NOPE - MLA_ROPE)], axis=1)
    cols = [dn_qkv, dn_z, jnp.repeat(dn_a, HEAD_DIM, axis=1), jnp.repeat(dn_b, HEAD_DIM, axis=1),
            cq, ckv, pe_slot(kpe), pe_slot(_rot_cols(kpe)),
            gq, gk, gv, gg, _pad_cols(glr, LANE),
            fq, fk, fv, _pad_cols(ff, LANE)]
    w = jnp.concatenate([_bf(c) for c in cols], axis=1)
    assert w.shape[1] == IN_TOTAL
    return w


def _layout_q_up(w):
    zeros = lambda n: jnp.zeros((w.shape[0], n), w.dtype)
    per = MLA_NOPE + MLA_ROPE
    plain, rot = [], []
    for h in range(HEADS):
        nope = w[:, h * per:h * per + MLA_NOPE]
        pe = w[:, h * per + MLA_NOPE:(h + 1) * per]
        plain += [nope, pe, zeros(SLOT - per)]
        rot += [zeros(MLA_NOPE), _rot_cols(pe), zeros(SLOT - per)]
    return _bf(jnp.concatenate(plain + rot, axis=1))


def _layout_kv_up(w):
    zeros = lambda n: jnp.zeros((w.shape[0], n), w.dtype)
    per = MLA_NOPE + HEAD_DIM
    ks, vs = [], []
    for h in range(HEADS):
        ks += [w[:, h * per:h * per + MLA_NOPE], zeros(SLOT - MLA_NOPE)]
        vs += [w[:, h * per + MLA_NOPE:(h + 1) * per]]
    return _bf(jnp.concatenate(ks + vs, axis=1))


def _rope_tables(seq):
    inv_freq = 1.0 / (ROPE_THETA ** (jnp.arange(0, MLA_ROPE, 2, dtype=F32) / MLA_ROPE))
    ang = jnp.arange(seq, dtype=F32)[:, None] * inv_freq[None, :]
    ang = jnp.concatenate([ang, ang], axis=-1)
    pad = jnp.zeros((seq, SLOT - MLA_NOPE - MLA_ROPE), F32)
    cos_t = jnp.concatenate([jnp.ones((seq, MLA_NOPE), F32), jnp.cos(ang), pad], axis=1)
    sin_t = jnp.concatenate([jnp.zeros((seq, MLA_NOPE), F32), jnp.sin(ang), pad], axis=1)
    return cos_t, sin_t


TOKEN_TILE = 512


def _layer(x2, batch, seq, p):
    ga, gc, q_b, k_b, vt_b, q_d, k_d, vt_d = _inproj(x2, p, seq, TOKEN_TILE)
    o_a = _deltanet(ga, p["alog"], p["dtb"], p["dn_norm"], seq)
    o_b = _flash(q_b, k_b, vt_b, batch, seq, True, TOKEN_TILE)
    o_c = _gla(gc, p["gla_norm"], seq)
    o_d = _flash(q_d, k_d, vt_d, batch, seq, False, TOKEN_TILE)
    x2 = _merge(x2, o_a, o_b, o_c, o_d, p["norm_mix_pre"], p["w_gate"], p["b_gate"],
                p["w_branch_up"], p["w_out"], p["norm_mix_post"])
    return _mlp(x2, p["norm_mlp_pre"], p["w_mlp_in"], p["w_mlp_out"], p["norm_mlp_post"])


def kernel(x, norm_mix_pre, norm_mix_post, norm_mlp_pre, norm_mlp_post, w_in, dn_conv, dn_a_log, dn_dt_bias, dn_norm, mla_q_norm, mla_w_q_up, mla_kv_norm, mla_w_kv_up, gla_w_gate_up, gla_gate_bias, gla_norm, fox_f_bias, w_branch_up, w_gate, b_gate, w_out, w_mlp_in, w_mlp_out):
    batch, seq, d = x.shape
    depth = w_in.shape[0]
    cos_t, sin_t = _rope_tables(seq)
    x2 = x.reshape(batch * seq, d)
    row = lambda a: a.reshape(1, -1)
    for l in range(depth):
        p = {
            "norm_mix_pre": row(norm_mix_pre[l]), "norm_mix_post": row(norm_mix_post[l]),
            "norm_mlp_pre": row(norm_mlp_pre[l]), "norm_mlp_post": row(norm_mlp_post[l]),
            "w_cat": _layout_in_weight(w_in[l]),
            "dn_conv": dn_conv[l],
            "wgu": _bf(jnp.pad(gla_w_gate_up[l], ((0, LANE - GLA_RANK), (0, 0)))),
            "gbias": row(gla_gate_bias[l]),
            "alog": row(jnp.repeat(dn_a_log[l], HEAD_DIM)), "dtb": row(jnp.repeat(dn_dt_bias[l], HEAD_DIM)),
            "dn_norm": row(jnp.tile(dn_norm[l], HEADS)),
            "mla_q_norm": row(mla_q_norm[l]), "mla_kv_norm": row(mla_kv_norm[l]),
            "wq": _layout_q_up(mla_w_q_up[l]), "wkv": _layout_kv_up(mla_w_kv_up[l]),
            "cos": cos_t, "sin": sin_t,
            "gla_norm": row(jnp.tile(gla_norm[l], HEADS)),
            "fbias": row(jnp.pad(fox_f_bias[l], (0, LANE - HEADS))),
            "w_gate": _bf(w_gate[l]), "b_gate": b_gate[l], "w_branch_up": _bf(w_branch_up[l]),
            "w_out": _bf(w_out[l]), "w_mlp_in": _bf(w_mlp_in[l]), "w_mlp_out": _bf(w_mlp_out[l]),
        }
        x2 = _layer(x2, batch, seq, p)
    return x2.reshape(batch, seq, d)
```

```python
import functools
import math

import jax
import jax.numpy as jnp
import numpy as np
from jax import lax
from jax.experimental import pallas as pl
from jax.experimental.pallas import tpu as pltpu

F32 = jnp.float32
BF16 = jnp.bfloat16

D_MODEL = 1024
CHUNK = 64
EPS = 1e-6
HEADS = 4
HEAD_DIM = 64
BRANCH_WIDTH = HEADS * HEAD_DIM
D_FF = 4 * D_MODEL
CONV_K = 4
MLA_NOPE = HEAD_DIM
MLA_ROPE = HEAD_DIM // 2
MLA_Q_LORA = D_MODEL // 4
MLA_KV_LORA = D_MODEL // 8
ROPE_THETA = 10000.0
GLA_DK = HEAD_DIM // 2
GLA_RANK = 16
GLA_TAU = 16.0
LOG2E = math.log2(math.e)
LANE = 128
SLOT = LANE

SM_A, SM_B, SM_LR, SM_F = 0, HEADS, 2 * HEADS, 2 * HEADS + GLA_RANK
A_MAIN = 4 * BRANCH_WIDTH
A_WIDTH = A_MAIN + LANE
B_WIDTH = MLA_Q_LORA + MLA_KV_LORA + SLOT
C_MAIN = 2 * HEADS * GLA_DK + 2 * BRANCH_WIDTH
C_WIDTH = C_MAIN + LANE
D_WIDTH = 3 * BRANCH_WIDTH
OFF_B = A_MAIN
OFF_C = OFF_B + B_WIDTH
OFF_SM = OFF_C + C_MAIN
OFF_D = OFF_SM + LANE
IN_TOTAL = OFF_D + D_WIDTH

VMEM_LIMIT = 56 * 1024 * 1024


def _dot(a, b):
    return jnp.dot(a, b, preferred_element_type=F32)


def _dot_nt(a, b):
    return lax.dot_general(a, b, (((1,), (1,)), ((), ())), preferred_element_type=F32)


def _dot_tn(a, b):
    return lax.dot_general(a, b, (((0,), (0,)), ((), ())), preferred_element_type=F32)


def _bf(x):
    return x.astype(BF16)


def _split(x, pieces):
    out = []
    for _ in range(pieces - 1):
        hi = _bf(x)
        out.append(hi)
        x = x - hi.astype(F32)
    out.append(_bf(x))
    return out


def _dot3(a, b):
    ah, al = _split(a, 2)
    bh, bl = _split(b, 2)
    return _dot(ah, bh) + (_dot(ah, bl) + _dot(al, bh))


def _dot_ones(ones_bf, x):
    x1, x2, x3 = _split(x, 3)
    return _dot(ones_bf, x1) + (_dot(ones_bf, x2) + _dot(ones_bf, x3))


def _rms(x, gain):
    return x * lax.rsqrt(jnp.mean(x * x, axis=-1, keepdims=True) + EPS) * gain


def _silu(x):
    return x * jax.nn.sigmoid(x)


def _softplus(x):
    return jnp.maximum(x, 0.0) + jnp.log1p(jnp.exp(-jnp.abs(x)))


def _log_sigmoid(x):
    return jnp.minimum(x, 0.0) - jnp.log1p(jnp.exp(-jnp.abs(x)))


def _tri(n, strict=False):
    r = lax.broadcasted_iota(jnp.int32, (n, n), 0)
    c = lax.broadcasted_iota(jnp.int32, (n, n), 1)
    return (c < r) if strict else (c <= r)


def _params(*sem):
    return pltpu.CompilerParams(dimension_semantics=sem, vmem_limit_bytes=VMEM_LIMIT)


def _const_spec(shape):
    nd = len(shape)
    return pl.BlockSpec(shape, lambda *_: (0,) * nd, pipeline_mode=pl.Buffered(1))


def _inproj_kernel(x_ref, gain_ref, w_ref, convw_ref, wgu_ref, gbias_ref,
                   qn_ref, kvn_ref, wq_ref, wkv_ref, cos_ref, sin_ref, fb_ref,
                   oa_ref, oc_ref, qb_ref, kb_ref, vtb_ref, qd_ref, kd_ref, vtd_ref,
                   xe_ref, carry_ref, *, tm, tiles_per_seq):
    i = pl.program_id(0)
    h = _bf(_rms(x_ref[...], gain_ref[...]))

    qkv_w = 3 * BRANCH_WIDTH
    cw = BRANCH_WIDTH

    @pl.when(i % tiles_per_seq == 0)
    def _():
        xe_ref[0:8, :] = jnp.zeros((8, qkv_w), F32)
        carry_ref[...] = jnp.zeros_like(carry_ref)

    def conv_cols(c0):
        y = jnp.zeros((tm, cw), F32)
        for j in range(CONV_K):
            y = y + convw_ref[j:j + 1, c0:c0 + cw] * xe_ref[pl.ds(8 - (CONV_K - 1) + j, tm), c0:c0 + cw]
        oa_ref[:, c0:c0 + cw] = _silu(y)
        xe_ref[0:8, c0:c0 + cw] = xe_ref[tm:tm + 8, c0:c0 + cw]

    pcs = _dot(h, w_ref[:, OFF_C:OFF_D])
    oc_ref[:, 0:C_MAIN] = pcs[:, 0:C_MAIN]
    small = pcs[:, C_MAIN:C_MAIN + LANE]
    oa_ref[:, A_MAIN:A_WIDTH] = small
    pd = _dot(h, w_ref[:, OFF_D:OFF_D + D_WIDTH])
    pb = _dot(h, w_ref[:, OFF_B:OFF_B + B_WIDTH])
    fox = _fox_stages(pd, small, fb_ref, carry_ref, qd_ref, kd_ref, vtd_ref)
    mla = _mla_stages(pb, qn_ref, kvn_ref, wq_ref, wkv_ref, cos_ref, sin_ref, qb_ref, kb_ref, vtb_ref)

    xe_ref[8:8 + tm, 0:2 * cw] = _dot(h, w_ref[:, 0:2 * cw])
    for stage in fox:
        stage()
    pa = _dot(h, w_ref[:, 2 * cw:4 * cw])
    xe_ref[8:8 + tm, 2 * cw:3 * cw] = pa[:, 0:cw]
    oa_ref[:, 3 * cw:4 * cw] = pa[:, cw:2 * cw]
    conv_cols(0)
    conv_cols(cw)
    conv_cols(2 * cw)
    for stage in mla:
        stage()
    gate = _dot(_bf(small), wgu_ref[...]) + gbias_ref[...]
    oc_ref[:, C_MAIN:C_WIDTH] = _log_sigmoid(gate) * (1.0 / GLA_TAU)


def _inproj(x2, p, seq, tm):
    t = x2.shape[0]
    tps = seq // tm
    nslot = HEADS * SLOT
    row = lambda w: pl.BlockSpec((tm, w), lambda i: (i, 0))
    pos = pl.BlockSpec((tm, SLOT), lambda i: (i % tps, 0))
    vt_spec = pl.BlockSpec((1, HEADS * VT_ROWS, tm), lambda i: (i, 0, 0))
    slots = jax.ShapeDtypeStruct((t, nslot), BF16)
    vts = jax.ShapeDtypeStruct((t // tm, HEADS * VT_ROWS, tm), BF16)
    return pl.pallas_call(
        functools.partial(_inproj_kernel, tm=tm, tiles_per_seq=tps),
        grid=(t // tm,),
        in_specs=[row(D_MODEL), _const_spec((1, D_MODEL)), _const_spec((D_MODEL, IN_TOTAL)),
                  _const_spec((CONV_K, 3 * BRANCH_WIDTH)), _const_spec((LANE, LANE)),
                  _const_spec((1, LANE)),
                  _const_spec((1, MLA_Q_LORA)), _const_spec((1, MLA_KV_LORA)),
                  _const_spec((MLA_Q_LORA, 2 * nslot)), _const_spec((MLA_KV_LORA, nslot + BRANCH_WIDTH)),
                  pos, pos, _const_spec((1, LANE))],
        out_specs=[row(A_WIDTH), row(C_WIDTH), row(nslot), row(nslot), vt_spec,
                   row(nslot), row(nslot), vt_spec],
        out_shape=[jax.ShapeDtypeStruct((t, A_WIDTH), F32), jax.ShapeDtypeStruct((t, C_WIDTH), F32),
                   slots, slots, vts, slots, slots, vts],
        scratch_shapes=[pltpu.VMEM((tm + 8, 3 * BRANCH_WIDTH), F32), pltpu.VMEM((1, LANE), F32)],
        compiler_params=_params("arbitrary"),
        name="inproj",
    )(x2, p["norm_mix_pre"], p["w_cat"], p["dn_conv"], p["wgu"], p["gbias"],
      p["mla_q_norm"], p["mla_kv_norm"], p["wq"], p["wkv"], p["cos"], p["sin"], p["fbias"])


def _head_ones():
    r = lax.broadcasted_iota(jnp.int32, (BRANCH_WIDTH, BRANCH_WIDTH), 0)
    c = lax.broadcasted_iota(jnp.int32, (BRANCH_WIDTH, BRANCH_WIDTH), 1)
    shift = HEAD_DIM.bit_length() - 1
    return _bf(jnp.where((r >> shift) == (c >> shift), 1.0, 0.0))


def _head_sums(x, ones_bd):
    x1, x2 = _split(x, 2)
    return _dot(x1, ones_bd) + _dot(x2, ones_bd)


def _head_rms(x, ones_bd, gain):
    return x * lax.rsqrt(_head_sums(x * x, ones_bd) * (1.0 / HEAD_DIM) + EPS) * gain


def _deltanet_kernel(a_ref, alog_ref, dtb_ref, norm_ref, o_ref, state_ref, *, tb, blocks_per_seq):
    i = pl.program_id(0)

    @pl.when(i % blocks_per_seq == 0)
    def _():
        state_ref[...] = jnp.zeros_like(state_ref)

    bw = BRANCH_WIDTH
    nchunk = tb // CHUNK
    hshift = HEAD_DIM.bit_length() - 1
    t_idx = lax.broadcasted_iota(jnp.int32, (CHUNK, bw), 0)
    lane = lax.broadcasted_iota(jnp.int32, (CHUNK, bw), 1)
    s_idx = lane & (HEAD_DIM - 1)
    lower = s_idx <= t_idx
    strict = s_idx < t_idx
    same_block = lambda log2s: (t_idx >> log2s) == (s_idx >> log2s)
    eye = jnp.where(s_idx == t_idx, 1.0, 0.0).astype(F32)
    strict_f = jnp.where(strict, 1.0, 0.0).astype(F32)
    head_sel = [_bf(jnp.where((lane >> hshift) == h, 1.0, 0.0)) for h in range(HEADS)]
    r64 = lax.broadcasted_iota(jnp.int32, (CHUNK, CHUNK), 0)
    c64 = lax.broadcasted_iota(jnp.int32, (CHUNK, CHUNK), 1)
    lmat = _bf(jnp.where(c64 <= r64, 1.0, 0.0))
    r256 = lax.broadcasted_iota(jnp.int32, (bw, bw), 0)
    c256 = lax.broadcasted_iota(jnp.int32, (bw, bw), 1)
    same_head = (r256 >> hshift) == (c256 >> hshift)
    ones_bd = _bf(jnp.where(same_head, 1.0, 0.0))

    def bd(y):
        return jnp.concatenate([y * m for m in head_sel], axis=0)

    def mm(a, b):
        return _dot(a, bd(b))

    def mm3(a, b):
        ah, al = _split(a, 2)
        bh, bl = _split(b, 2)
        bdh = bd(bh)
        return _dot(ah, bdh) + (_dot(ah, bd(bl)) + _dot(al, bdh))

    small = a_ref[:, A_MAIN:A_WIDTH]
    g_small = -jnp.exp(alog_ref[...]) * _softplus(small + dtb_ref[...])
    b_small = jax.nn.sigmoid(small)
    head_of_lane = lax.broadcasted_iota(jnp.int32, (tb, bw), 1) >> hshift

    def per_head(x, lane0):
        out = jnp.broadcast_to(x[:, lane0 + HEADS - 1:lane0 + HEADS], (tb, bw))
        for h in range(HEADS - 2, -1, -1):
            out = jnp.where(head_of_lane == h, jnp.broadcast_to(x[:, lane0 + h:lane0 + h + 1], (tb, bw)), out)
        return out

    g_all = per_head(g_small, SM_A)
    beta_all = per_head(b_small, SM_B)
    q_all = a_ref[:, 0:bw]
    k_all = a_ref[:, bw:2 * bw]
    q_all = q_all * lax.rsqrt(_head_sums(q_all * q_all, ones_bd) + EPS) * (HEAD_DIM ** -0.5)
    k_all = k_all * lax.rsqrt(_head_sums(k_all * k_all, ones_bd) + EPS)
    vb_all = a_ref[:, 2 * bw:3 * bw] * beta_all

    chunks = range(nchunk)
    rows = lambda x, c: x[c * CHUNK:(c + 1) * CHUNK, :]
    qs = [rows(q_all, c) for c in chunks]
    ks = [rows(k_all, c) for c in chunks]
    gs = [rows(g_all, c) for c in chunks]
    betas = [rows(beta_all, c) for c in chunks]

    cums = []
    for g in gs:
        g1, g2 = _split(jnp.concatenate([g * strict_f, g], axis=1), 2)
        cums.append(_dot(lmat, g1) + _dot(lmat, g2))
    diffs = [cm[:, 0:bw] for cm in cums]
    gcs = [cm[:, bw:2 * bw] for cm in cums]
    decays = [jnp.where(lower, jnp.exp(jnp.where(lower, d, 0.0)), 0.0) for d in diffs]
    kbs = [_bf(k) for k in ks]
    kqs = [_dot_nt(jnp.concatenate([kb, _bf(q)], axis=0), bd(kb)) for kb, q in zip(kbs, qs)]
    amats = [jnp.where(strict, b * kq[0:CHUNK] * d, 0.0) for b, kq, d in zip(betas, kqs, decays)]

    xs = [eye - jnp.where(same_block(1), a, 0.0) for a in amats]
    for log2s in range(1, CHUNK.bit_length() - 1):
        off = same_block(log2s + 1) & ~same_block(log2s)
        xbs = [_bf(x) for x in xs]
        xns = [_bf(mm(xb, _bf(jnp.where(off, a, 0.0)))) for xb, a in zip(xbs, amats)]
        xs = [x - mm(xn, xb) for x, xn, xb in zip(xs, xns, xbs)]
    resid = [eye - mm3(eye + a, x) for a, x in zip(amats, xs)]
    xs = [x + mm(_bf(x), _bf(r)) for x, r in zip(xs, resid)]

    egcs = [jnp.exp(gc) for gc in gcs]
    us = [mm3(x, rows(vb_all, c)) for x, c in zip(xs, chunks)]
    wbs = [_bf(mm(_bf(x), _bf(k * b * e))) for x, k, b, e in zip(xs, ks, betas, egcs)]
    qks = [_bf(kq[CHUNK:2 * CHUNK] * d) for kq, d in zip(kqs, decays)]
    q_decays = [_bf(q * e) for q, e in zip(qs, egcs)]
    lasts = [gc[CHUNK - 1:CHUNK, :] for gc in gcs]
    k_tails = [_bf(k * jnp.exp(l - gc)) for k, l, gc in zip(ks, lasts, gcs)]
    cds = [jnp.exp(l) for l in lasts]

    s = state_ref[...]
    o_rows = []
    for c in chunks:
        sb = _bf(s)
        vb = _bf(us[c] - _dot(wbs[c], sb))
        o_rows.append(_dot(q_decays[c], sb) + mm(qks[c], vb))
        s = s * cds[c] + jnp.where(same_head, _dot_tn(k_tails[c], vb), 0.0)
    state_ref[...] = s
    o_all = jnp.concatenate(o_rows, axis=0)
    o_ref[...] = _bf(_head_rms(o_all, ones_bd, norm_ref[...]) * _silu(a_ref[:, 3 * bw:4 * bw]))


def _deltanet(ga, alog, dtb, norm, seq, tb=512):
    t = ga.shape[0]
    return pl.pallas_call(
        functools.partial(_deltanet_kernel, tb=tb, blocks_per_seq=seq // tb),
        grid=(t // tb,),
        in_specs=[pl.BlockSpec((tb, A_WIDTH), lambda i: (i, 0)),
                  _const_spec((1, LANE)), _const_spec((1, LANE)),
                  _const_spec((1, BRANCH_WIDTH))],
        out_specs=pl.BlockSpec((tb, BRANCH_WIDTH), lambda i: (i, 0)),
        out_shape=jax.ShapeDtypeStruct((t, BRANCH_WIDTH), BF16),
        scratch_shapes=[pltpu.VMEM((BRANCH_WIDTH, BRANCH_WIDTH), F32)],
        compiler_params=_params("arbitrary"),
        name="deltanet",
    )(ga, alog, dtb, norm)


def _gla_kernel(c_ref, norm_ref, o_ref, state_ref, *, tb, blocks_per_seq):
    i = pl.program_id(0)

    @pl.when(i % blocks_per_seq == 0)
    def _():
        state_ref[...] = jnp.zeros_like(state_ref)

    qw = HEADS * GLA_DK
    bw = BRANCH_WIDTH
    nchunk = tb // CHUNK
    la_off = 2 * qw + 2 * bw
    kshift = GLA_DK.bit_length() - 1
    vshift = HEAD_DIM.bit_length() - 1
    r64 = lax.broadcasted_iota(jnp.int32, (CHUNK, CHUNK), 0)
    c64 = lax.broadcasted_iota(jnp.int32, (CHUNK, CHUNK), 1)
    lmat = _bf(jnp.where(c64 <= r64, 1.0, 0.0))
    t_idx = lax.broadcasted_iota(jnp.int32, (CHUNK, bw), 0)
    lane_v = lax.broadcasted_iota(jnp.int32, (CHUNK, bw), 1)
    lane_k = lax.broadcasted_iota(jnp.int32, (CHUNK, qw), 1)
    lower = (lane_v & (HEAD_DIM - 1)) <= t_idx
    sel_v = [_bf(jnp.where((lane_v >> vshift) == h, 1.0, 0.0)) for h in range(HEADS)]
    sel_k = [_bf(jnp.where((lane_k >> kshift) == h, 1.0, 0.0)) for h in range(HEADS)]
    same_head = (lax.broadcasted_iota(jnp.int32, (bw, qw), 0) >> vshift) == \
                (lax.broadcasted_iota(jnp.int32, (bw, qw), 1) >> kshift)

    def bd(y, sel):
        return jnp.concatenate([y * m for m in sel], axis=0)

    chunks = range(nchunk)
    rows = lambda col0, width, c: c_ref[c * CHUNK:(c + 1) * CHUNK, col0:col0 + width]

    bcums = []
    for c in chunks:
        a1, a2 = _split(rows(la_off, qw, c), 2)
        bcums.append(_dot(lmat, a1) + _dot(lmat, a2))
    q_ins = [_bf(rows(0, qw, c) * (GLA_DK ** -0.5) * jnp.exp(b)) for c, b in zip(chunks, bcums)]
    k_ins = [_bf(rows(qw, qw, c) * jnp.exp(-b)) for c, b in zip(chunks, bcums)]
    vbs = [_bf(rows(2 * qw, bw, c)) for c in chunks]
    attns = [_bf(jnp.where(lower, _dot_nt(qi, bd(ki, sel_k)), 0.0)) for qi, ki in zip(q_ins, k_ins)]
    intras = [_dot(a, bd(vb, sel_v)) for a, vb in zip(attns, vbs)]
    lasts = [b[CHUNK - 1:CHUNK, :] for b in bcums]
    k_tails = [_bf(rows(qw, qw, c) * jnp.exp(l - b)) for c, l, b in zip(chunks, lasts, bcums)]
    cds = [jnp.exp(l) for l in lasts]

    st = state_ref[...]
    o_rows = []
    for c in chunks:
        o_rows.append(intras[c] + _dot_nt(q_ins[c], _bf(st)))
        st = st * cds[c] + jnp.where(same_head, _dot_tn(vbs[c], k_tails[c]), 0.0)
    state_ref[...] = st
    o_all = jnp.concatenate(o_rows, axis=0)
    gate = c_ref[:, 2 * qw + bw:2 * qw + 2 * bw]
    o_ref[...] = _bf(_head_rms(o_all, _head_ones(), norm_ref[...]) * _silu(gate))


def _gla(gc, norm, seq, tb=512):
    t = gc.shape[0]
    return pl.pallas_call(
        functools.partial(_gla_kernel, tb=tb, blocks_per_seq=seq // tb),
        grid=(t // tb,),
        in_specs=[pl.BlockSpec((tb, C_WIDTH), lambda i: (i, 0)), _const_spec((1, BRANCH_WIDTH))],
        out_specs=pl.BlockSpec((tb, BRANCH_WIDTH), lambda i: (i, 0)),
        out_shape=jax.ShapeDtypeStruct((t, BRANCH_WIDTH), BF16),
        scratch_shapes=[pltpu.VMEM((BRANCH_WIDTH, HEADS * GLA_DK), F32)],
        compiler_params=_params("arbitrary"),
        name="gla",
    )(gc, norm)


VT_ROWS = HEAD_DIM + 16


def _store_vt(vt_ref, v):
    vt = v.T
    tokens = v.shape[0]
    pad_rows = VT_ROWS - HEAD_DIM
    ones_row = _bf(jnp.where(lax.broadcasted_iota(jnp.int32, (pad_rows, tokens), 0) == 0, 1.0, 0.0))
    for h in range(HEADS):
        vt_ref[0, h * VT_ROWS:h * VT_ROWS + HEAD_DIM, :] = _bf(vt[h * HEAD_DIM:(h + 1) * HEAD_DIM, :])
        vt_ref[0, h * VT_ROWS + HEAD_DIM:(h + 1) * VT_ROWS, :] = ones_row


def _mla_stages(pb, qn_ref, kvn_ref, wq_ref, wkv_ref, cos_ref, sin_ref, q_ref, k_ref, vt_ref):
    nslot = HEADS * SLOT
    st = {}

    def prologue():
        cq = pb[:, 0:MLA_Q_LORA]
        ckv = pb[:, MLA_Q_LORA:MLA_Q_LORA + MLA_KV_LORA]
        off = MLA_Q_LORA + MLA_KV_LORA
        st["cos"] = cos_ref[...]
        st["sin"] = sin_ref[...]
        st["qa"] = _dot(_bf(_rms(cq, qn_ref[...])), wq_ref[...])
        st["kv"] = _dot(_bf(_rms(ckv, kvn_ref[...])), wkv_ref[...])
        kslot = pb[:, off:off + SLOT]
        st["k_pe"] = kslot * st["cos"] + pltpu.roll(kslot, SLOT - MLA_ROPE, axis=1) * st["sin"]

    def head(h):
        def run():
            sl = slice(h * SLOT, (h + 1) * SLOT)
            qa = st["qa"]
            q_rope = qa[:, sl] * st["cos"] + qa[:, nslot + h * SLOT:nslot + (h + 1) * SLOT] * st["sin"]
            q_ref[:, sl] = _bf(q_rope * ((MLA_NOPE + MLA_ROPE) ** -0.5 * LOG2E))
            k_ref[:, sl] = _bf(st["kv"][:, sl] + st["k_pe"])
        return run

    def store_v():
        _store_vt(vt_ref, st["kv"][:, nslot:nslot + BRANCH_WIDTH])

    return [prologue] + [head(h) for h in range(HEADS)] + [store_v]


def _fox_stages(pd, small, fb_ref, carry_ref, q_ref, k_ref, vt_ref):
    tb = pd.shape[0]
    bw = BRANCH_WIDTH
    scale = HEAD_DIM ** -0.5 * LOG2E
    st = {}

    def prologue():
        log_f = _log_sigmoid(small + fb_ref[...])
        lmat = _bf(jnp.where(_tri(tb), 1.0, 0.0))
        fcum = _dot_ones(lmat, log_f) + carry_ref[...]
        carry_ref[...] = fcum[tb - 1:tb, :]
        st["f"] = [f.astype(F32) for f in _split(fcum * LOG2E, 3)]

    def head(h):
        def run():
            lane = lax.broadcasted_iota(jnp.int32, (tb, LANE), 1)
            pair, odd = divmod(h, 2)
            qp = pd[:, pair * LANE:(pair + 1) * LANE] * scale
            kp = pd[:, bw + pair * LANE:bw + (pair + 1) * LANE]
            head_lanes = (lane >= HEAD_DIM) if odd else (lane < HEAD_DIM)
            a0 = 0 if odd else HEAD_DIM
            fb = [jnp.broadcast_to(f[:, SM_F + h:SM_F + h + 1], (tb, LANE)) for f in st["f"]]
            qx = jnp.zeros((tb, LANE), F32)
            kx = jnp.zeros((tb, LANE), F32)
            for j in range(3):
                qx = jnp.where(lane == a0 + j, fb[j], qx)
                qx = jnp.where(lane == a0 + 3 + j, 1.0, qx)
                kx = jnp.where(lane == a0 + j, 1.0, kx)
                kx = jnp.where(lane == a0 + 3 + j, -fb[j], kx)
            q_ref[:, h * SLOT:(h + 1) * SLOT] = _bf(jnp.where(head_lanes, qp, qx))
            k_ref[:, h * SLOT:(h + 1) * SLOT] = _bf(jnp.where(head_lanes, kp, kx))
        return run

    def store_v():
        _store_vt(vt_ref, pd[:, 2 * bw:3 * bw])

    return [prologue] + [head(h) for h in range(HEADS)] + [store_v]


def _flash_kernel(q_ref, k_ref, vt_ref, o_ref, m_ref, acc_ref, s_ref, *, chunk_mask, tq, tk):
    assert tq == tk
    qi = pl.program_id(1)
    m_ref[...] = jnp.full_like(m_ref, -jnp.inf)
    acc_ref[...] = jnp.zeros_like(acc_ref)

    def scores(ki, slot, hh):
        s_ref[slot, hh] = _dot_nt(k_ref[ki, :, hh * SLOT:(hh + 1) * SLOT],
                                  q_ref[:, hh * SLOT:(hh + 1) * SLOT])

    def softmax_pv(ki, slot, hh, masked):
        s = s_ref[slot, hh]
        if masked:
            kpos = lax.broadcasted_iota(jnp.int32, s.shape, 0)
            qpos = lax.broadcasted_iota(jnp.int32, s.shape, 1)
            if chunk_mask:
                shift = CHUNK.bit_length() - 1
                ok = (kpos >> shift) <= (qpos >> shift)
            else:
                ok = kpos <= qpos
            s = jnp.where(ok, s, -jnp.inf)
        m_prev = m_ref[hh]
        m_new = jnp.maximum(m_prev, jnp.max(s, axis=0, keepdims=True))
        p = _bf(jnp.exp2(s - m_new))
        pv = _dot(vt_ref[ki, hh * VT_ROWS:(hh + 1) * VT_ROWS, :], p)
        acc_ref[hh] = jnp.exp2(m_prev - m_new) * acc_ref[hh] + pv
        m_ref[hh] = m_new

    def step(ki, slot, masked=False, prefetch=True):
        for hh in range(HEADS):
            if prefetch:
                scores(ki + 1, 1 - slot, hh)
            softmax_pv(ki, slot, hh, masked)

    for hh in range(HEADS):
        scores(0, 0, hh)

    def pair(j, carry):
        step(2 * j, 0)
        step(2 * j + 1, 1)
        return carry

    lax.fori_loop(0, qi // 2, pair, 0)

    @pl.when(qi % 2 == 0)
    def _():
        step(qi, 0, masked=True, prefetch=False)

    @pl.when(qi % 2 == 1)
    def _():
        step(qi - 1, 0)
        step(qi, 1, masked=True, prefetch=False)

    ot = jnp.concatenate([acc_ref[hh][0:HEAD_DIM] / acc_ref[hh][HEAD_DIM:HEAD_DIM + 1]
                          for hh in range(HEADS)], axis=0)
    o_ref[...] = _bf(ot.T)


def _flash(q, k, vt3, batch, seq, chunk_mask, tq):
    tk = tq
    nq = seq // tq
    nk = seq // tk
    t = q.shape[0]
    k3 = k.reshape(t // tk, tk, HEADS * SLOT)
    return pl.pallas_call(
        functools.partial(_flash_kernel, chunk_mask=chunk_mask, tq=tq, tk=tk),
        grid=(batch, nq),
        in_specs=[pl.BlockSpec((tq, HEADS * SLOT), lambda b, qi: (b * nq + qi, 0)),
                  pl.BlockSpec((nk, tk, HEADS * SLOT), lambda b, qi: (b, 0, 0)),
                  pl.BlockSpec((nk, HEADS * VT_ROWS, tk), lambda b, qi: (b, 0, 0))],
        out_specs=pl.BlockSpec((tq, BRANCH_WIDTH), lambda b, qi: (b * nq + qi, 0)),
        out_shape=jax.ShapeDtypeStruct((t, BRANCH_WIDTH), BF16),
        scratch_shapes=[pltpu.VMEM((HEADS, 1, tq), F32), pltpu.VMEM((HEADS, VT_ROWS, tq), F32),
                        pltpu.VMEM((2, HEADS, tk, tq), F32)],
        compiler_params=_params("parallel", "arbitrary"),
        name="flash_chunk" if chunk_mask else "flash_causal",
    )(q, k3, vt3)


def _merge_kernel(x_ref, oa_ref, ob_ref, oc_ref, od_ref, gpre_ref, wg_ref, bg_ref, wup_ref,
                  wout_ref, gpost_ref, o_ref):
    x = x_ref[...]
    h = _bf(_rms(x, gpre_ref[...]))
    merged = jnp.zeros(x.shape, F32)
    for i, br in enumerate((oa_ref, ob_ref, oc_ref, od_ref)):
        gate = jax.nn.sigmoid(_dot(h, wg_ref[i]) + bg_ref[i:i + 1, :])
        merged = merged + gate * _dot(br[...], wup_ref[i])
    y = _dot(_bf(merged), wout_ref[...])
    o_ref[...] = x + _rms(y, gpost_ref[...])


def _merge(x2, oa, ob, oc, od, gpre, wg, bg, wup, wout, gpost, tm=512):
    t = x2.shape[0]
    row = lambda w: pl.BlockSpec((tm, w), lambda i: (i, 0))
    return pl.pallas_call(
        _merge_kernel,
        grid=(t // tm,),
        in_specs=[row(D_MODEL)] + [row(BRANCH_WIDTH)] * 4 +
                 [_const_spec((1, D_MODEL)), _const_spec((4, D_MODEL, D_MODEL)),
                  _const_spec((4, D_MODEL)), _const_spec((4, BRANCH_WIDTH, D_MODEL)),
                  _const_spec((D_MODEL, D_MODEL)), _const_spec((1, D_MODEL))],
        out_specs=row(D_MODEL),
        out_shape=jax.ShapeDtypeStruct((t, D_MODEL), F32),
        compiler_params=_params("parallel"),
        name="merge",
    )(x2, oa, ob, oc, od, gpre, wg, bg, wup, wout, gpost)


def _mlp_kernel(x_ref, gpre_ref, w1_ref, w2_ref, gpost_ref, o_ref, *, ff_chunk):
    x = x_ref[...]
    h = _bf(_rms(x, gpre_ref[...]))
    y = jnp.zeros(x.shape, F32)
    for c in range(D_FF // ff_chunk):
        u = jnp.maximum(_dot(h, w1_ref[:, c * ff_chunk:(c + 1) * ff_chunk]), 0.0)
        y = y + _dot(_bf(u * u), w2_ref[c * ff_chunk:(c + 1) * ff_chunk, :])
    o_ref[...] = x + _rms(y, gpost_ref[...])


def _mlp(x2, gpre, w1, w2, gpost, tm=512, ff_chunk=1024):
    t = x2.shape[0]
    row = pl.BlockSpec((tm, D_MODEL), lambda i: (i, 0))
    return pl.pallas_call(
        functools.partial(_mlp_kernel, ff_chunk=ff_chunk),
        grid=(t // tm,),
        in_specs=[row, _const_spec((1, D_MODEL)), _const_spec((D_MODEL, D_FF)),
                  _const_spec((D_FF, D_MODEL)), _const_spec((1, D_MODEL))],
        out_specs=row,
        out_shape=jax.ShapeDtypeStruct((t, D_MODEL), F32),
        compiler_params=_params("parallel"),
        name="mlp",
    )(x2, gpre, w1, w2, gpost)


_IN_SIZES = (
    HEADS * 3 * HEAD_DIM, HEADS * HEAD_DIM, HEADS, HEADS,
    MLA_Q_LORA, MLA_KV_LORA, MLA_ROPE,
    HEADS * GLA_DK, HEADS * GLA_DK, HEADS * HEAD_DIM, HEADS * HEAD_DIM, GLA_RANK,
    HEADS * HEAD_DIM, HEADS * HEAD_DIM, HEADS * HEAD_DIM, HEADS,
)


def _rot_cols(w):
    half = w.shape[-1] // 2
    return jnp.concatenate([-w[..., half:], w[..., :half]], axis=-1)


def _pad_cols(w, width):
    return jnp.pad(w, ((0, 0), (0, width - w.shape[-1])))


def _layout_in_weight(w_in):
    splits = np.cumsum(_IN_SIZES)[:-1].tolist()
    (dn_qkv, dn_z, dn_b, dn_a, cq, ckv, kpe, gq, gk, gv, gg, glr, fq, fk, fv, ff) = jnp.split(w_in, splits, axis=1)
    zeros = lambda n: jnp.zeros((w_in.shape[0], n), w_in.dtype)
    assert (SM_A, SM_B, SM_LR, SM_F) == (0, HEADS, 2 * HEADS, 2 * HEADS + GLA_RANK)
    small = _pad_cols(jnp.concatenate([dn_a, dn_b, glr, ff], axis=1), LANE)
    cols = [dn_qkv, dn_z,
            cq, ckv, zeros(MLA_NOPE), kpe, _rot_cols(kpe),
            gq, gk, gv, gg, small,
            fq, fk, fv]
    w = jnp.concatenate([_bf(c) for c in cols], axis=1)
    assert w.shape[1] == IN_TOTAL
    return w


def _layout_q_up(w):
    zeros = lambda n: jnp.zeros((w.shape[0], n), w.dtype)
    per = MLA_NOPE + MLA_ROPE
    plain, rot = [], []
    for h in range(HEADS):
        nope = w[:, h * per:h * per + MLA_NOPE]
        pe = w[:, h * per + MLA_NOPE:(h + 1) * per]
        plain += [nope, pe, zeros(SLOT - per)]
        rot += [zeros(MLA_NOPE), _rot_cols(pe), zeros(SLOT - per)]
    return _bf(jnp.concatenate(plain + rot, axis=1))


def _layout_kv_up(w):
    zeros = lambda n: jnp.zeros((w.shape[0], n), w.dtype)
    per = MLA_NOPE + HEAD_DIM
    ks, vs = [], []
    for h in range(HEADS):
        ks += [w[:, h * per:h * per + MLA_NOPE], zeros(SLOT - MLA_NOPE)]
        vs += [w[:, h * per + MLA_NOPE:(h + 1) * per]]
    return _bf(jnp.concatenate(ks + vs, axis=1))


def _rope_tables(seq):
    inv_freq = 1.0 / (ROPE_THETA ** (jnp.arange(0, MLA_ROPE, 2, dtype=F32) / MLA_ROPE))
    ang = jnp.arange(seq, dtype=F32)[:, None] * inv_freq[None, :]
    ang = jnp.concatenate([ang, ang], axis=-1)
    pad = jnp.zeros((seq, SLOT - MLA_NOPE - MLA_ROPE), F32)
    cos_t = jnp.concatenate([jnp.ones((seq, MLA_NOPE), F32), jnp.cos(ang), pad], axis=1)
    sin_t = jnp.concatenate([jnp.zeros((seq, MLA_NOPE), F32), jnp.sin(ang), pad], axis=1)
    return cos_t, sin_t


TOKEN_TILE = 512


def _layer(x2, batch, seq, p):
    ga, gc, q_b, k_b, vt_b, q_d, k_d, vt_d = _inproj(x2, p, seq, TOKEN_TILE)
    o_a = _deltanet(ga, p["alog"], p["dtb"], p["dn_norm"], seq)
    o_b = _flash(q_b, k_b, vt_b, batch, seq, True, TOKEN_TILE)
    o_c = _gla(gc, p["gla_norm"], seq)
    o_d = _flash(q_d, k_d, vt_d, batch, seq, False, TOKEN_TILE)
    x2 = _merge(x2, o_a, o_b, o_c, o_d, p["norm_mix_pre"], p["w_gate"], p["b_gate"],
                p["w_branch_up"], p["w_out"], p["norm_mix_post"])
    return _mlp(x2, p["norm_mlp_pre"], p["w_mlp_in"], p["w_mlp_out"], p["norm_mlp_post"])


def kernel(x, norm_mix_pre, norm_mix_post, norm_mlp_pre, norm_mlp_post, w_in, dn_conv, dn_a_log, dn_dt_bias, dn_norm, mla_q_norm, mla_w_q_up, mla_kv_norm, mla_w_kv_up, gla_w_gate_up, gla_gate_bias, gla_norm, fox_f_bias, w_branch_up, w_gate, b_gate, w_out, w_mlp_in, w_mlp_out):
    batch, seq, d = x.shape
    depth = w_in.shape[0]
    cos_t, sin_t = _rope_tables(seq)
    x2 = x.reshape(batch * seq, d)
    row = lambda a: a.reshape(1, -1)
    lanes_at = lambda a, lane0: row(jnp.pad(a, (lane0, LANE - lane0 - a.shape[0])))
    for l in range(depth):
        p = {
            "norm_mix_pre": row(norm_mix_pre[l]), "norm_mix_post": row(norm_mix_post[l]),
            "norm_mlp_pre": row(norm_mlp_pre[l]), "norm_mlp_post": row(norm_mlp_post[l]),
            "w_cat": _layout_in_weight(w_in[l]),
            "dn_conv": dn_conv[l],
            "wgu": _bf(jnp.pad(gla_w_gate_up[l], ((SM_LR, LANE - SM_LR - GLA_RANK), (0, 0)))),
            "gbias": row(gla_gate_bias[l]),
            "alog": lanes_at(dn_a_log[l], SM_A), "dtb": lanes_at(dn_dt_bias[l], SM_A),
            "dn_norm": row(jnp.tile(dn_norm[l], HEADS)),
            "mla_q_norm": row(mla_q_norm[l]), "mla_kv_norm": row(mla_kv_norm[l]),
            "wq": _layout_q_up(mla_w_q_up[l]), "wkv": _layout_kv_up(mla_w_kv_up[l]),
            "cos": cos_t, "sin": sin_t,
            "gla_norm": row(jnp.tile(gla_norm[l], HEADS)),
            "fbias": lanes_at(fox_f_bias[l], SM_F),
            "w_gate": _bf(w_gate[l]), "b_gate": b_gate[l], "w_branch_up": _bf(w_branch_up[l]),
            "w_out": _bf(w_out[l]), "w_mlp_in": _bf(w_mlp_in[l]), "w_mlp_out": _bf(w_mlp_out[l]),
        }
        x2 = _layer(x2, batch, seq, p)
    return x2.reshape(batch, seq, d)
```

```python
import functools
import math

import jax
import jax.numpy as jnp
import numpy as np
from jax import lax
from jax.experimental import pallas as pl
from jax.experimental.pallas import tpu as pltpu

F32 = jnp.float32
BF16 = jnp.bfloat16

D_MODEL = 1024
CHUNK = 64
EPS = 1e-6
HEADS = 4
HEAD_DIM = 64
BRANCH_WIDTH = HEADS * HEAD_DIM
D_FF = 4 * D_MODEL
CONV_K = 4
MLA_NOPE = HEAD_DIM
MLA_ROPE = HEAD_DIM // 2
MLA_Q_LORA = D_MODEL // 4
MLA_KV_LORA = D_MODEL // 8
ROPE_THETA = 10000.0
GLA_DK = HEAD_DIM // 2
GLA_RANK = 16
GLA_TAU = 16.0
LOG2E = math.log2(math.e)
LANE = 128
SLOT = LANE

SM_A, SM_B, SM_LR, SM_F = 0, HEADS, 2 * HEADS, 2 * HEADS + GLA_RANK
A_MAIN = 4 * BRANCH_WIDTH
A_WIDTH = A_MAIN + LANE
B_WIDTH = MLA_Q_LORA + MLA_KV_LORA + SLOT
C_MAIN = 2 * HEADS * GLA_DK + 2 * BRANCH_WIDTH
C_WIDTH = C_MAIN + LANE
D_WIDTH = 3 * BRANCH_WIDTH
OFF_B = A_MAIN
OFF_C = OFF_B + B_WIDTH
OFF_SM = OFF_C + C_MAIN
OFF_D = OFF_SM + LANE
IN_TOTAL = OFF_D + D_WIDTH

VMEM_LIMIT = 56 * 1024 * 1024


def _dot(a, b):
    return jnp.dot(a, b, preferred_element_type=F32)


def _dot_nt(a, b):
    return lax.dot_general(a, b, (((1,), (1,)), ((), ())), preferred_element_type=F32)


def _dot_tn(a, b):
    return lax.dot_general(a, b, (((0,), (0,)), ((), ())), preferred_element_type=F32)


def _bf(x):
    return x.astype(BF16)


def _split(x, pieces):
    out = []
    for _ in range(pieces - 1):
        hi = _bf(x)
        out.append(hi)
        x = x - hi.astype(F32)
    out.append(_bf(x))
    return out


def _dot3(a, b):
    ah, al = _split(a, 2)
    bh, bl = _split(b, 2)
    return _dot(ah, bh) + (_dot(ah, bl) + _dot(al, bh))


def _dot_ones(ones_bf, x):
    x1, x2, x3 = _split(x, 3)
    return _dot(ones_bf, x1) + (_dot(ones_bf, x2) + _dot(ones_bf, x3))


def _rms(x, gain):
    return x * lax.rsqrt(jnp.mean(x * x, axis=-1, keepdims=True) + EPS) * gain


def _silu(x):
    return x * jax.nn.sigmoid(x)


def _softplus(x):
    return jnp.maximum(x, 0.0) + jnp.log1p(jnp.exp(-jnp.abs(x)))


def _log_sigmoid(x):
    return jnp.minimum(x, 0.0) - jnp.log1p(jnp.exp(-jnp.abs(x)))


def _tri(n, strict=False):
    r = lax.broadcasted_iota(jnp.int32, (n, n), 0)
    c = lax.broadcasted_iota(jnp.int32, (n, n), 1)
    return (c < r) if strict else (c <= r)


def _params(*sem):
    return pltpu.CompilerParams(dimension_semantics=sem, vmem_limit_bytes=VMEM_LIMIT)


def _const_spec(shape):
    nd = len(shape)
    return pl.BlockSpec(shape, lambda *_: (0,) * nd, pipeline_mode=pl.Buffered(1))


def _inproj_kernel(x_ref, gain_ref, w_ref, convw_ref, wgu_ref, gbias_ref,
                   qn_ref, kvn_ref, wq_ref, wkv_ref, cos_ref, sin_ref, fb_ref,
                   oa_ref, oc_ref, qb_ref, kb_ref, vtb_ref, qd_ref, kd_ref, vtd_ref,
                   xe_ref, carry_ref, *, tm, tiles_per_seq):
    i = pl.program_id(0)
    h = _bf(_rms(x_ref[...], gain_ref[...]))

    qkv_w = 3 * BRANCH_WIDTH
    cw = BRANCH_WIDTH

    @pl.when(i % tiles_per_seq == 0)
    def _():
        xe_ref[0:8, :] = jnp.zeros((8, qkv_w), F32)
        carry_ref[...] = jnp.zeros_like(carry_ref)

    def conv_cols(c0):
        y = jnp.zeros((tm, cw), F32)
        for j in range(CONV_K):
            y = y + convw_ref[j:j + 1, c0:c0 + cw] * xe_ref[pl.ds(8 - (CONV_K - 1) + j, tm), c0:c0 + cw]
        oa_ref[:, c0:c0 + cw] = _silu(y)
        xe_ref[0:8, c0:c0 + cw] = xe_ref[tm:tm + 8, c0:c0 + cw]

    pcs = _dot(h, w_ref[:, OFF_C:OFF_D])
    oc_ref[:, 0:C_MAIN] = pcs[:, 0:C_MAIN]
    small = pcs[:, C_MAIN:C_MAIN + LANE]
    oa_ref[:, A_MAIN:A_WIDTH] = small
    pd = _dot(h, w_ref[:, OFF_D:OFF_D + D_WIDTH])
    pb = _dot(h, w_ref[:, OFF_B:OFF_B + B_WIDTH])
    fox = _fox_stages(pd, small, fb_ref, carry_ref, qd_ref, kd_ref, vtd_ref)
    mla = _mla_stages(pb, qn_ref, kvn_ref, wq_ref, wkv_ref, cos_ref, sin_ref, qb_ref, kb_ref, vtb_ref)

    xe_ref[8:8 + tm, 0:2 * cw] = _dot(h, w_ref[:, 0:2 * cw])
    for stage in fox:
        stage()
    pa = _dot(h, w_ref[:, 2 * cw:4 * cw])
    xe_ref[8:8 + tm, 2 * cw:3 * cw] = pa[:, 0:cw]
    oa_ref[:, 3 * cw:4 * cw] = pa[:, cw:2 * cw]
    conv_cols(0)
    conv_cols(cw)
    conv_cols(2 * cw)
    for stage in mla:
        stage()
    gate = _dot(_bf(small), wgu_ref[...]) + gbias_ref[...]
    oc_ref[:, C_MAIN:C_WIDTH] = _log_sigmoid(gate) * (1.0 / GLA_TAU)


def _inproj(x2, p, seq, tm):
    t = x2.shape[0]
    tps = seq // tm
    nslot = HEADS * SLOT
    row = lambda w: pl.BlockSpec((tm, w), lambda i: (i, 0))
    pos = pl.BlockSpec((tm, SLOT), lambda i: (i % tps, 0))
    vt_spec = pl.BlockSpec((1, HEADS * VT_ROWS, tm), lambda i: (i, 0, 0))
    slots = jax.ShapeDtypeStruct((t, nslot), BF16)
    vts = jax.ShapeDtypeStruct((t // tm, HEADS * VT_ROWS, tm), BF16)
    return pl.pallas_call(
        functools.partial(_inproj_kernel, tm=tm, tiles_per_seq=tps),
        grid=(t // tm,),
        in_specs=[row(D_MODEL), _const_spec((1, D_MODEL)), _const_spec((D_MODEL, IN_TOTAL)),
                  _const_spec((CONV_K, 3 * BRANCH_WIDTH)), _const_spec((LANE, LANE)),
                  _const_spec((1, LANE)),
                  _const_spec((1, MLA_Q_LORA)), _const_spec((1, MLA_KV_LORA)),
                  _const_spec((MLA_Q_LORA, 2 * nslot)), _const_spec((MLA_KV_LORA, nslot + BRANCH_WIDTH)),
                  pos, pos, _const_spec((1, LANE))],
        out_specs=[row(A_WIDTH), row(C_WIDTH), row(nslot), row(nslot), vt_spec,
                   row(nslot), row(nslot), vt_spec],
        out_shape=[jax.ShapeDtypeStruct((t, A_WIDTH), F32), jax.ShapeDtypeStruct((t, C_WIDTH), F32),
                   slots, slots, vts, slots, slots, vts],
        scratch_shapes=[pltpu.VMEM((tm + 8, 3 * BRANCH_WIDTH), F32), pltpu.VMEM((1, LANE), F32)],
        compiler_params=_params("arbitrary"),
        name="inproj",
    )(x2, p["norm_mix_pre"], p["w_cat"], p["dn_conv"], p["wgu"], p["gbias"],
      p["mla_q_norm"], p["mla_kv_norm"], p["wq"], p["wkv"], p["cos"], p["sin"], p["fbias"])


def _head_ones():
    r = lax.broadcasted_iota(jnp.int32, (BRANCH_WIDTH, BRANCH_WIDTH), 0)
    c = lax.broadcasted_iota(jnp.int32, (BRANCH_WIDTH, BRANCH_WIDTH), 1)
    shift = HEAD_DIM.bit_length() - 1
    return _bf(jnp.where((r >> shift) == (c >> shift), 1.0, 0.0))


def _head_sums(x, ones_bd):
    x1, x2 = _split(x, 2)
    return _dot(x1, ones_bd) + _dot(x2, ones_bd)


def _head_rms(x, ones_bd, gain):
    return x * lax.rsqrt(_head_sums(x * x, ones_bd) * (1.0 / HEAD_DIM) + EPS) * gain


def _deltanet_kernel(a_ref, c_ref, alog_ref, dtb_ref, norm_ref, gla_norm_ref, o_ref, oc_ref,
                     state_ref, gla_state_ref, *, tb, blocks_per_seq):
    i = pl.program_id(0)

    @pl.when(i % blocks_per_seq == 0)
    def _():
        state_ref[...] = jnp.zeros_like(state_ref)
        gla_state_ref[...] = jnp.zeros_like(gla_state_ref)

    gla_stages, gla_chunk_step, gla_finish = _gla_parts(c_ref, gla_norm_ref, oc_ref, gla_state_ref, tb)

    bw = BRANCH_WIDTH
    nchunk = tb // CHUNK
    hshift = HEAD_DIM.bit_length() - 1
    t_idx = lax.broadcasted_iota(jnp.int32, (CHUNK, bw), 0)
    lane = lax.broadcasted_iota(jnp.int32, (CHUNK, bw), 1)
    s_idx = lane & (HEAD_DIM - 1)
    lower = s_idx <= t_idx
    strict = s_idx < t_idx
    same_block = lambda log2s: (t_idx >> log2s) == (s_idx >> log2s)
    eye = jnp.where(s_idx == t_idx, 1.0, 0.0).astype(F32)
    strict_f = jnp.where(strict, 1.0, 0.0).astype(F32)
    head_sel = [_bf(jnp.where((lane >> hshift) == h, 1.0, 0.0)) for h in range(HEADS)]
    r64 = lax.broadcasted_iota(jnp.int32, (CHUNK, CHUNK), 0)
    c64 = lax.broadcasted_iota(jnp.int32, (CHUNK, CHUNK), 1)
    lmat = _bf(jnp.where(c64 <= r64, 1.0, 0.0))
    r256 = lax.broadcasted_iota(jnp.int32, (bw, bw), 0)
    c256 = lax.broadcasted_iota(jnp.int32, (bw, bw), 1)
    same_head = (r256 >> hshift) == (c256 >> hshift)
    ones_bd = _bf(jnp.where(same_head, 1.0, 0.0))

    def bd(y):
        return jnp.concatenate([y * m for m in head_sel], axis=0)

    def mm(a, b):
        return _dot(a, bd(b))

    def mm3(a, b):
        ah, al = _split(a, 2)
        bh, bl = _split(b, 2)
        bdh = bd(bh)
        return _dot(ah, bdh) + (_dot(ah, bd(bl)) + _dot(al, bdh))

    small = a_ref[:, A_MAIN:A_WIDTH]
    g_small = -jnp.exp(alog_ref[...]) * _softplus(small + dtb_ref[...])
    b_small = jax.nn.sigmoid(small)
    head_of_lane = lax.broadcasted_iota(jnp.int32, (tb, bw), 1) >> hshift

    def per_head(x, lane0):
        out = jnp.broadcast_to(x[:, lane0 + HEADS - 1:lane0 + HEADS], (tb, bw))
        for h in range(HEADS - 2, -1, -1):
            out = jnp.where(head_of_lane == h, jnp.broadcast_to(x[:, lane0 + h:lane0 + h + 1], (tb, bw)), out)
        return out

    g_all = per_head(g_small, SM_A)
    beta_all = per_head(b_small, SM_B)
    q_all = a_ref[:, 0:bw]
    k_all = a_ref[:, bw:2 * bw]
    q_all = q_all * lax.rsqrt(_head_sums(q_all * q_all, ones_bd) + EPS) * (HEAD_DIM ** -0.5)
    k_all = k_all * lax.rsqrt(_head_sums(k_all * k_all, ones_bd) + EPS)
    vb_all = a_ref[:, 2 * bw:3 * bw] * beta_all

    chunks = range(nchunk)
    rows = lambda x, c: x[c * CHUNK:(c + 1) * CHUNK, :]
    qs = [rows(q_all, c) for c in chunks]
    ks = [rows(k_all, c) for c in chunks]
    gs = [rows(g_all, c) for c in chunks]
    betas = [rows(beta_all, c) for c in chunks]

    cums = []
    for g in gs:
        g1, g2 = _split(jnp.concatenate([g * strict_f, g], axis=1), 2)
        cums.append(_dot(lmat, g1) + _dot(lmat, g2))
    diffs = [cm[:, 0:bw] for cm in cums]
    gcs = [cm[:, bw:2 * bw] for cm in cums]
    decays = [jnp.where(lower, jnp.exp(jnp.where(lower, d, 0.0)), 0.0) for d in diffs]
    kbs = [_bf(k) for k in ks]
    kqs = [_dot_nt(jnp.concatenate([kb, _bf(q)], axis=0), bd(kb)) for kb, q in zip(kbs, qs)]
    amats = [jnp.where(strict, b * kq[0:CHUNK] * d, 0.0) for b, kq, d in zip(betas, kqs, decays)]

    xs = [eye - jnp.where(same_block(1), a, 0.0) for a in amats]
    for log2s in range(1, CHUNK.bit_length() - 1):
        off = same_block(log2s + 1) & ~same_block(log2s)
        xbs = [_bf(x) for x in xs]
        xns = [_bf(mm(xb, _bf(jnp.where(off, a, 0.0)))) for xb, a in zip(xbs, amats)]
        if log2s <= len(gla_stages):
            gla_stages[log2s - 1]()
        xs = [x - mm(xn, xb) for x, xn, xb in zip(xs, xns, xbs)]
    resid = [eye - mm3(eye + a, x) for a, x in zip(amats, xs)]
    xs = [x + mm(_bf(x), _bf(r)) for x, r in zip(xs, resid)]

    egcs = [jnp.exp(gc) for gc in gcs]
    us = [mm3(x, rows(vb_all, c)) for x, c in zip(xs, chunks)]
    wbs = [_bf(mm(_bf(x), _bf(k * b * e))) for x, k, b, e in zip(xs, ks, betas, egcs)]
    qks = [_bf(kq[CHUNK:2 * CHUNK] * d) for kq, d in zip(kqs, decays)]
    q_decays = [_bf(q * e) for q, e in zip(qs, egcs)]
    lasts = [gc[CHUNK - 1:CHUNK, :] for gc in gcs]
    k_tails = [_bf(k * jnp.exp(l - gc)) for k, l, gc in zip(ks, lasts, gcs)]
    cds = [jnp.exp(l) for l in lasts]

    s = state_ref[...]
    o_rows = []
    for c in chunks:
        sb = _bf(s)
        vb = _bf(us[c] - _dot(wbs[c], sb))
        gla_chunk_step(c)
        o_rows.append(_dot(q_decays[c], sb) + mm(qks[c], vb))
        s = s * cds[c] + jnp.where(same_head, _dot_tn(k_tails[c], vb), 0.0)
    state_ref[...] = s
    gla_finish()
    o_all = jnp.concatenate(o_rows, axis=0)
    o_ref[...] = _bf(_head_rms(o_all, ones_bd, norm_ref[...]) * _silu(a_ref[:, 3 * bw:4 * bw]))


def _recurrent_mixers(ga, gc, alog, dtb, dn_norm, gla_norm, seq, tb=512):
    t = ga.shape[0]
    out = jax.ShapeDtypeStruct((t, BRANCH_WIDTH), BF16)
    out_spec = pl.BlockSpec((tb, BRANCH_WIDTH), lambda i: (i, 0))
    return pl.pallas_call(
        functools.partial(_deltanet_kernel, tb=tb, blocks_per_seq=seq // tb),
        grid=(t // tb,),
        in_specs=[pl.BlockSpec((tb, A_WIDTH), lambda i: (i, 0)),
                  pl.BlockSpec((tb, C_WIDTH), lambda i: (i, 0)),
                  _const_spec((1, LANE)), _const_spec((1, LANE)),
                  _const_spec((1, BRANCH_WIDTH)), _const_spec((1, BRANCH_WIDTH))],
        out_specs=[out_spec, out_spec],
        out_shape=[out, out],
        scratch_shapes=[pltpu.VMEM((BRANCH_WIDTH, BRANCH_WIDTH), F32),
                        pltpu.VMEM((BRANCH_WIDTH, HEADS * GLA_DK), F32)],
        compiler_params=_params("arbitrary"),
        name="deltanet_gla",
    )(ga, gc, alog, dtb, dn_norm, gla_norm)


def _gla_parts(c_ref, norm_ref, o_ref, state_ref, tb):
    qw = HEADS * GLA_DK
    bw = BRANCH_WIDTH
    nchunk = tb // CHUNK
    la_off = 2 * qw + 2 * bw
    kshift = GLA_DK.bit_length() - 1
    vshift = HEAD_DIM.bit_length() - 1
    r64 = lax.broadcasted_iota(jnp.int32, (CHUNK, CHUNK), 0)
    c64 = lax.broadcasted_iota(jnp.int32, (CHUNK, CHUNK), 1)
    lmat = _bf(jnp.where(c64 <= r64, 1.0, 0.0))
    t_idx = lax.broadcasted_iota(jnp.int32, (CHUNK, bw), 0)
    lane_v = lax.broadcasted_iota(jnp.int32, (CHUNK, bw), 1)
    lane_k = lax.broadcasted_iota(jnp.int32, (CHUNK, qw), 1)
    lower = (lane_v & (HEAD_DIM - 1)) <= t_idx
    sel_v = [_bf(jnp.where((lane_v >> vshift) == h, 1.0, 0.0)) for h in range(HEADS)]
    sel_k = [_bf(jnp.where((lane_k >> kshift) == h, 1.0, 0.0)) for h in range(HEADS)]
    same_head = (lax.broadcasted_iota(jnp.int32, (bw, qw), 0) >> vshift) == \
                (lax.broadcasted_iota(jnp.int32, (bw, qw), 1) >> kshift)

    def bd(y, sel):
        return jnp.concatenate([y * m for m in sel], axis=0)

    chunks = range(nchunk)
    rows = lambda col0, width, c: c_ref[c * CHUNK:(c + 1) * CHUNK, col0:col0 + width]

    v = {}

    def cumulative_gate():
        v["bcum"] = []
        for c in chunks:
            a1, a2 = _split(rows(la_off, qw, c), 2)
            v["bcum"].append(_dot(lmat, a1) + _dot(lmat, a2))

    def scaled_operands():
        v["q_in"] = [_bf(rows(0, qw, c) * (GLA_DK ** -0.5) * jnp.exp(b)) for c, b in zip(chunks, v["bcum"])]
        v["k_in"] = [_bf(rows(qw, qw, c) * jnp.exp(-b)) for c, b in zip(chunks, v["bcum"])]
        v["vb"] = [_bf(rows(2 * qw, bw, c)) for c in chunks]

    def intra_scores():
        v["attn"] = [_bf(jnp.where(lower, _dot_nt(qi, bd(ki, sel_k)), 0.0))
                     for qi, ki in zip(v["q_in"], v["k_in"])]

    def intra_outputs():
        v["intra"] = [_dot(a, bd(vb, sel_v)) for a, vb in zip(v["attn"], v["vb"])]
        lasts = [b[CHUNK - 1:CHUNK, :] for b in v["bcum"]]
        v["k_tail"] = [_bf(rows(qw, qw, c) * jnp.exp(l - b)) for c, l, b in zip(chunks, lasts, v["bcum"])]
        v["cd"] = [jnp.exp(l) for l in lasts]
        v["state"] = state_ref[...]
        v["o_rows"] = []

    def chunk_step(c):
        st = v["state"]
        v["o_rows"].append(v["intra"][c] + _dot_nt(v["q_in"][c], _bf(st)))
        v["state"] = st * v["cd"][c] + jnp.where(same_head, _dot_tn(v["vb"][c], v["k_tail"][c]), 0.0)

    def finish():
        state_ref[...] = v["state"]
        o_all = jnp.concatenate(v["o_rows"], axis=0)
        gate = c_ref[:, 2 * qw + bw:2 * qw + 2 * bw]
        o_ref[...] = _bf(_head_rms(o_all, _head_ones(), norm_ref[...]) * _silu(gate))

    return [cumulative_gate, scaled_operands, intra_scores, intra_outputs], chunk_step, finish


VT_ROWS = HEAD_DIM + 16


def _store_vt(vt_ref, v):
    vt = v.T
    tokens = v.shape[0]
    pad_rows = VT_ROWS - HEAD_DIM
    ones_row = _bf(jnp.where(lax.broadcasted_iota(jnp.int32, (pad_rows, tokens), 0) == 0, 1.0, 0.0))
    for h in range(HEADS):
        vt_ref[0, h * VT_ROWS:h * VT_ROWS + HEAD_DIM, :] = _bf(vt[h * HEAD_DIM:(h + 1) * HEAD_DIM, :])
        vt_ref[0, h * VT_ROWS + HEAD_DIM:(h + 1) * VT_ROWS, :] = ones_row


def _mla_stages(pb, qn_ref, kvn_ref, wq_ref, wkv_ref, cos_ref, sin_ref, q_ref, k_ref, vt_ref):
    nslot = HEADS * SLOT
    st = {}

    def prologue():
        cq = pb[:, 0:MLA_Q_LORA]
        ckv = pb[:, MLA_Q_LORA:MLA_Q_LORA + MLA_KV_LORA]
        off = MLA_Q_LORA + MLA_KV_LORA
        st["cos"] = cos_ref[...]
        st["sin"] = sin_ref[...]
        st["qa"] = _dot(_bf(_rms(cq, qn_ref[...])), wq_ref[...])
        st["kv"] = _dot(_bf(_rms(ckv, kvn_ref[...])), wkv_ref[...])
        kslot = pb[:, off:off + SLOT]
        st["k_pe"] = kslot * st["cos"] + pltpu.roll(kslot, SLOT - MLA_ROPE, axis=1) * st["sin"]

    def head(h):
        def run():
            sl = slice(h * SLOT, (h + 1) * SLOT)
            qa = st["qa"]
            q_rope = qa[:, sl] * st["cos"] + qa[:, nslot + h * SLOT:nslot + (h + 1) * SLOT] * st["sin"]
            q_ref[:, sl] = _bf(q_rope * ((MLA_NOPE + MLA_ROPE) ** -0.5 * LOG2E))
            k_ref[:, sl] = _bf(st["kv"][:, sl] + st["k_pe"])
        return run

    def store_v():
        _store_vt(vt_ref, st["kv"][:, nslot:nslot + BRANCH_WIDTH])

    return [prologue] + [head(h) for h in range(HEADS)] + [store_v]


def _fox_stages(pd, small, fb_ref, carry_ref, q_ref, k_ref, vt_ref):
    tb = pd.shape[0]
    bw = BRANCH_WIDTH
    scale = HEAD_DIM ** -0.5 * LOG2E
    st = {}

    def prologue():
        log_f = _log_sigmoid(small + fb_ref[...])
        lmat = _bf(jnp.where(_tri(tb), 1.0, 0.0))
        fcum = _dot_ones(lmat, log_f) + carry_ref[...]
        carry_ref[...] = fcum[tb - 1:tb, :]
        st["f"] = [f.astype(F32) for f in _split(fcum * LOG2E, 3)]

    def head(h):
        def run():
            lane = lax.broadcasted_iota(jnp.int32, (tb, LANE), 1)
            pair, odd = divmod(h, 2)
            qp = pd[:, pair * LANE:(pair + 1) * LANE] * scale
            kp = pd[:, bw + pair * LANE:bw + (pair + 1) * LANE]
            head_lanes = (lane >= HEAD_DIM) if odd else (lane < HEAD_DIM)
            a0 = 0 if odd else HEAD_DIM
            fb = [jnp.broadcast_to(f[:, SM_F + h:SM_F + h + 1], (tb, LANE)) for f in st["f"]]
            lane_row = lax.broadcasted_iota(jnp.int32, (1, LANE), 1)
            at = lambda lo, n=1: jnp.where((lane_row >= lo) & (lane_row < lo + n), 1.0, 0.0).astype(F32)
            qx = fb[0] * at(a0) + fb[1] * at(a0 + 1) + fb[2] * at(a0 + 2) + at(a0 + 3, 3)
            kx = at(a0, 3) - (fb[0] * at(a0 + 3) + fb[1] * at(a0 + 4) + fb[2] * at(a0 + 5))
            q_ref[:, h * SLOT:(h + 1) * SLOT] = _bf(jnp.where(head_lanes, qp, qx))
            k_ref[:, h * SLOT:(h + 1) * SLOT] = _bf(jnp.where(head_lanes, kp, kx))
        return run

    def store_v():
        _store_vt(vt_ref, pd[:, 2 * bw:3 * bw])

    return [prologue] + [head(h) for h in range(HEADS)] + [store_v]


def _flash_kernel(q_ref, k_ref, vt_ref, o_ref, m_ref, acc_ref, s_ref, *, chunk_mask, tq, tk):
    assert tq == tk
    qi = pl.program_id(1)
    m_ref[...] = jnp.full_like(m_ref, -jnp.inf)
    acc_ref[...] = jnp.zeros_like(acc_ref)

    def scores(ki, slot, hh):
        s_ref[slot, hh] = _dot_nt(k_ref[ki, :, hh * SLOT:(hh + 1) * SLOT],
                                  q_ref[:, hh * SLOT:(hh + 1) * SLOT])

    def softmax_pv(ki, slot, hh, masked):
        s = s_ref[slot, hh]
        if masked:
            kpos = lax.broadcasted_iota(jnp.int32, s.shape, 0)
            qpos = lax.broadcasted_iota(jnp.int32, s.shape, 1)
            if chunk_mask:
                shift = CHUNK.bit_length() - 1
                ok = (kpos >> shift) <= (qpos >> shift)
            else:
                ok = kpos <= qpos
            s = jnp.where(ok, s, -jnp.inf)
        m_prev = m_ref[hh]
        m_new = jnp.maximum(m_prev, jnp.max(s, axis=0, keepdims=True))
        p = _bf(jnp.exp2(s - m_new))
        pv = _dot(vt_ref[ki, hh * VT_ROWS:(hh + 1) * VT_ROWS, :], p)
        acc_ref[hh] = jnp.exp2(m_prev - m_new) * acc_ref[hh] + pv
        m_ref[hh] = m_new

    def step(ki, slot, masked=False, prefetch=True):
        for hh in range(HEADS):
            if prefetch:
                scores(ki + 1, 1 - slot, hh)
            softmax_pv(ki, slot, hh, masked)

    for hh in range(HEADS):
        scores(0, 0, hh)

    def pair(j, carry):
        step(2 * j, 0)
        step(2 * j + 1, 1)
        return carry

    lax.fori_loop(0, qi // 2, pair, 0)

    @pl.when(qi % 2 == 0)
    def _():
        step(qi, 0, masked=True, prefetch=False)

    @pl.when(qi % 2 == 1)
    def _():
        step(qi - 1, 0)
        step(qi, 1, masked=True, prefetch=False)

    ot = jnp.concatenate([acc_ref[hh][0:HEAD_DIM] / acc_ref[hh][HEAD_DIM:HEAD_DIM + 1]
                          for hh in range(HEADS)], axis=0)
    o_ref[...] = _bf(ot.T)


def _flash(q, k, vt3, batch, seq, chunk_mask, tq):
    tk = tq
    nq = seq // tq
    nk = seq // tk
    t = q.shape[0]
    k3 = k.reshape(t // tk, tk, HEADS * SLOT)
    return pl.pallas_call(
        functools.partial(_flash_kernel, chunk_mask=chunk_mask, tq=tq, tk=tk),
        grid=(batch, nq),
        in_specs=[pl.BlockSpec((tq, HEADS * SLOT), lambda b, qi: (b * nq + qi, 0)),
                  pl.BlockSpec((nk, tk, HEADS * SLOT), lambda b, qi: (b, 0, 0)),
                  pl.BlockSpec((nk, HEADS * VT_ROWS, tk), lambda b, qi: (b, 0, 0))],
        out_specs=pl.BlockSpec((tq, BRANCH_WIDTH), lambda b, qi: (b * nq + qi, 0)),
        out_shape=jax.ShapeDtypeStruct((t, BRANCH_WIDTH), BF16),
        scratch_shapes=[pltpu.VMEM((HEADS, 1, tq), F32), pltpu.VMEM((HEADS, VT_ROWS, tq), F32),
                        pltpu.VMEM((2, HEADS, tk, tq), F32)],
        compiler_params=_params("parallel", "arbitrary"),
        name="flash_chunk" if chunk_mask else "flash_causal",
    )(q, k3, vt3)


def _merge_kernel(x_ref, oa_ref, ob_ref, oc_ref, od_ref, gpre_ref, wg_ref, bg_ref, wup_ref,
                  wout_ref, gpost_ref, o_ref):
    x = x_ref[...]
    h = _bf(_rms(x, gpre_ref[...]))
    merged = jnp.zeros(x.shape, F32)
    for i, br in enumerate((oa_ref, ob_ref, oc_ref, od_ref)):
        gate = jax.nn.sigmoid(_dot(h, wg_ref[i]) + bg_ref[i:i + 1, :])
        merged = merged + gate * _dot(br[...], wup_ref[i])
    y = _dot(_bf(merged), wout_ref[...])
    o_ref[...] = x + _rms(y, gpost_ref[...])


def _merge(x2, oa, ob, oc, od, gpre, wg, bg, wup, wout, gpost, tm=512):
    t = x2.shape[0]
    row = lambda w: pl.BlockSpec((tm, w), lambda i: (i, 0))
    return pl.pallas_call(
        _merge_kernel,
        grid=(t // tm,),
        in_specs=[row(D_MODEL)] + [row(BRANCH_WIDTH)] * 4 +
                 [_const_spec((1, D_MODEL)), _const_spec((4, D_MODEL, D_MODEL)),
                  _const_spec((4, D_MODEL)), _const_spec((4, BRANCH_WIDTH, D_MODEL)),
                  _const_spec((D_MODEL, D_MODEL)), _const_spec((1, D_MODEL))],
        out_specs=row(D_MODEL),
        out_shape=jax.ShapeDtypeStruct((t, D_MODEL), F32),
        compiler_params=_params("parallel"),
        name="merge",
    )(x2, oa, ob, oc, od, gpre, wg, bg, wup, wout, gpost)


def _mlp_kernel(x_ref, gpre_ref, w1_ref, w2_ref, gpost_ref, o_ref, *, ff_chunk):
    x = x_ref[...]
    h = _bf(_rms(x, gpre_ref[...]))
    y = jnp.zeros(x.shape, F32)
    for c in range(D_FF // ff_chunk):
        u = jnp.maximum(_dot(h, w1_ref[:, c * ff_chunk:(c + 1) * ff_chunk]), 0.0)
        y = y + _dot(_bf(u * u), w2_ref[c * ff_chunk:(c + 1) * ff_chunk, :])
    o_ref[...] = x + _rms(y, gpost_ref[...])


def _mlp(x2, gpre, w1, w2, gpost, tm=512, ff_chunk=1024):
    t = x2.shape[0]
    row = pl.BlockSpec((tm, D_MODEL), lambda i: (i, 0))
    return pl.pallas_call(
        functools.partial(_mlp_kernel, ff_chunk=ff_chunk),
        grid=(t // tm,),
        in_specs=[row, _const_spec((1, D_MODEL)), _const_spec((D_MODEL, D_FF)),
                  _const_spec((D_FF, D_MODEL)), _const_spec((1, D_MODEL))],
        out_specs=row,
        out_shape=jax.ShapeDtypeStruct((t, D_MODEL), F32),
        compiler_params=_params("parallel"),
        name="mlp",
    )(x2, gpre, w1, w2, gpost)


_IN_SIZES = (
    HEADS * 3 * HEAD_DIM, HEADS * HEAD_DIM, HEADS, HEADS,
    MLA_Q_LORA, MLA_KV_LORA, MLA_ROPE,
    HEADS * GLA_DK, HEADS * GLA_DK, HEADS * HEAD_DIM, HEADS * HEAD_DIM, GLA_RANK,
    HEADS * HEAD_DIM, HEADS * HEAD_DIM, HEADS * HEAD_DIM, HEADS,
)


def _rot_cols(w):
    half = w.shape[-1] // 2
    return jnp.concatenate([-w[..., half:], w[..., :half]], axis=-1)


def _pad_cols(w, width):
    return jnp.pad(w, ((0, 0), (0, width - w.shape[-1])))


def _layout_in_weight(w_in):
    splits = np.cumsum(_IN_SIZES)[:-1].tolist()
    (dn_qkv, dn_z, dn_b, dn_a, cq, ckv, kpe, gq, gk, gv, gg, glr, fq, fk, fv, ff) = jnp.split(w_in, splits, axis=1)
    zeros = lambda n: jnp.zeros((w_in.shape[0], n), w_in.dtype)
    assert (SM_A, SM_B, SM_LR, SM_F) == (0, HEADS, 2 * HEADS, 2 * HEADS + GLA_RANK)
    small = _pad_cols(jnp.concatenate([dn_a, dn_b, glr, ff], axis=1), LANE)
    cols = [dn_qkv, dn_z,
            cq, ckv, zeros(MLA_NOPE), kpe, _rot_cols(kpe),
            gq, gk, gv, gg, small,
            fq, fk, fv]
    w = jnp.concatenate([_bf(c) for c in cols], axis=1)
    assert w.shape[1] == IN_TOTAL
    return w


def _layout_q_up(w):
    zeros = lambda n: jnp.zeros((w.shape[0], n), w.dtype)
    per = MLA_NOPE + MLA_ROPE
    plain, rot = [], []
    for h in range(HEADS):
        nope = w[:, h * per:h * per + MLA_NOPE]
        pe = w[:, h * per + MLA_NOPE:(h + 1) * per]
        plain += [nope, pe, zeros(SLOT - per)]
        rot += [zeros(MLA_NOPE), _rot_cols(pe), zeros(SLOT - per)]
    return _bf(jnp.concatenate(plain + rot, axis=1))


def _layout_kv_up(w):
    zeros = lambda n: jnp.zeros((w.shape[0], n), w.dtype)
    per = MLA_NOPE + HEAD_DIM
    ks, vs = [], []
    for h in range(HEADS):
        ks += [w[:, h * per:h * per + MLA_NOPE], zeros(SLOT - MLA_NOPE)]
        vs += [w[:, h * per + MLA_NOPE:(h + 1) * per]]
    return _bf(jnp.concatenate(ks + vs, axis=1))


def _rope_tables(seq):
    inv_freq = 1.0 / (ROPE_THETA ** (jnp.arange(0, MLA_ROPE, 2, dtype=F32) / MLA_ROPE))
    ang = jnp.arange(seq, dtype=F32)[:, None] * inv_freq[None, :]
    ang = jnp.concatenate([ang, ang], axis=-1)
    pad = jnp.zeros((seq, SLOT - MLA_NOPE - MLA_ROPE), F32)
    cos_t = jnp.concatenate([jnp.ones((seq, MLA_NOPE), F32), jnp.cos(ang), pad], axis=1)
    sin_t = jnp.concatenate([jnp.zeros((seq, MLA_NOPE), F32), jnp.sin(ang), pad], axis=1)
    return cos_t, sin_t


TOKEN_TILE = 512


def _layer(x2, batch, seq, p):
    ga, gc, q_b, k_b, vt_b, q_d, k_d, vt_d = _inproj(x2, p, seq, TOKEN_TILE)
    o_a, o_c = _recurrent_mixers(ga, gc, p["alog"], p["dtb"], p["dn_norm"], p["gla_norm"], seq)
    o_b = _flash(q_b, k_b, vt_b, batch, seq, True, TOKEN_TILE)
    o_d = _flash(q_d, k_d, vt_d, batch, seq, False, TOKEN_TILE)
    x2 = _merge(x2, o_a, o_b, o_c, o_d, p["norm_mix_pre"], p["w_gate"], p["b_gate"],
                p["w_branch_up"], p["w_out"], p["norm_mix_post"])
    return _mlp(x2, p["norm_mlp_pre"], p["w_mlp_in"], p["w_mlp_out"], p["norm_mlp_post"])


def kernel(x, norm_mix_pre, norm_mix_post, norm_mlp_pre, norm_mlp_post, w_in, dn_conv, dn_a_log, dn_dt_bias, dn_norm, mla_q_norm, mla_w_q_up, mla_kv_norm, mla_w_kv_up, gla_w_gate_up, gla_gate_bias, gla_norm, fox_f_bias, w_branch_up, w_gate, b_gate, w_out, w_mlp_in, w_mlp_out):
    batch, seq, d = x.shape
    depth = w_in.shape[0]
    cos_t, sin_t = _rope_tables(seq)
    x2 = x.reshape(batch * seq, d)
    row = lambda a: a.reshape(1, -1)
    lanes_at = lambda a, lane0: row(jnp.pad(a, (lane0, LANE - lane0 - a.shape[0])))
    for l in range(depth):
        p = {
            "norm_mix_pre": row(norm_mix_pre[l]), "norm_mix_post": row(norm_mix_post[l]),
            "norm_mlp_pre": row(norm_mlp_pre[l]), "norm_mlp_post": row(norm_mlp_post[l]),
            "w_cat": _layout_in_weight(w_in[l]),
            "dn_conv": dn_conv[l],
            "wgu": _bf(jnp.pad(gla_w_gate_up[l], ((SM_LR, LANE - SM_LR - GLA_RANK), (0, 0)))),
            "gbias": row(gla_gate_bias[l]),
            "alog": lanes_at(dn_a_log[l], SM_A), "dtb": lanes_at(dn_dt_bias[l], SM_A),
            "dn_norm": row(jnp.tile(dn_norm[l], HEADS)),
            "mla_q_norm": row(mla_q_norm[l]), "mla_kv_norm": row(mla_kv_norm[l]),
            "wq": _layout_q_up(mla_w_q_up[l]), "wkv": _layout_kv_up(mla_w_kv_up[l]),
            "cos": cos_t, "sin": sin_t,
            "gla_norm": row(jnp.tile(gla_norm[l], HEADS)),
            "fbias": lanes_at(fox_f_bias[l], SM_F),
            "w_gate": _bf(w_gate[l]), "b_gate": b_gate[l], "w_branch_up": _bf(w_branch_up[l]),
            "w_out": _bf(w_out[l]), "w_mlp_in": _bf(w_mlp_in[l]), "w_mlp_out": _bf(w_mlp_out[l]),
        }
        x2 = _layer(x2, batch, seq, p)
    return x2.reshape(batch, seq, d)
```

```python
import functools
import math

import jax
import jax.numpy as jnp
import numpy as np
from jax import lax
from jax.experimental import pallas as pl
from jax.experimental.pallas import tpu as pltpu

F32 = jnp.float32
BF16 = jnp.bfloat16

D_MODEL = 1024
CHUNK = 64
EPS = 1e-6
HEADS = 4
HEAD_DIM = 64
BRANCH_WIDTH = HEADS * HEAD_DIM
D_FF = 4 * D_MODEL
CONV_K = 4
MLA_NOPE = HEAD_DIM
MLA_ROPE = HEAD_DIM // 2
MLA_Q_LORA = D_MODEL // 4
MLA_KV_LORA = D_MODEL // 8
ROPE_THETA = 10000.0
GLA_DK = HEAD_DIM // 2
GLA_RANK = 16
GLA_TAU = 16.0
LOG2E = math.log2(math.e)
LANE = 128
SLOT = LANE

SM_A, SM_B, SM_LR, SM_F = 0, HEADS, 2 * HEADS, 2 * HEADS + GLA_RANK
A_MAIN = 4 * BRANCH_WIDTH
A_WIDTH = A_MAIN + LANE
B_WIDTH = MLA_Q_LORA + MLA_KV_LORA + SLOT
C_MAIN = 2 * HEADS * GLA_DK + 2 * BRANCH_WIDTH
C_WIDTH = C_MAIN + LANE
D_WIDTH = 3 * BRANCH_WIDTH
OFF_B = A_MAIN
OFF_C = OFF_B + B_WIDTH
OFF_SM = OFF_C + C_MAIN
OFF_D = OFF_SM + LANE
IN_TOTAL = OFF_D + D_WIDTH

VMEM_LIMIT = 56 * 1024 * 1024


def _dot(a, b):
    return jnp.dot(a, b, preferred_element_type=F32)


def _dot_nt(a, b):
    return lax.dot_general(a, b, (((1,), (1,)), ((), ())), preferred_element_type=F32)


def _dot_tn(a, b):
    return lax.dot_general(a, b, (((0,), (0,)), ((), ())), preferred_element_type=F32)


def _bf(x):
    return x.astype(BF16)


def _split(x, pieces):
    out = []
    for _ in range(pieces - 1):
        hi = _bf(x)
        out.append(hi)
        x = x - hi.astype(F32)
    out.append(_bf(x))
    return out


def _dot_ones(ones_bf, x):
    x1, x2, x3 = _split(x, 3)
    return _dot(ones_bf, x1) + (_dot(ones_bf, x2) + _dot(ones_bf, x3))


def _rms(x, gain):
    return x * lax.rsqrt(jnp.mean(x * x, axis=-1, keepdims=True) + EPS) * gain


def _silu(x):
    return x * jax.nn.sigmoid(x)


def _softplus(x):
    return jnp.maximum(x, 0.0) + jnp.log1p(jnp.exp(-jnp.abs(x)))


def _log_sigmoid(x):
    return jnp.minimum(x, 0.0) - jnp.log1p(jnp.exp(-jnp.abs(x)))


def _tri(n, strict=False):
    r = lax.broadcasted_iota(jnp.int32, (n, n), 0)
    c = lax.broadcasted_iota(jnp.int32, (n, n), 1)
    return (c < r) if strict else (c <= r)


def _params(*sem):
    return pltpu.CompilerParams(dimension_semantics=sem, vmem_limit_bytes=VMEM_LIMIT)


def _const_spec(shape):
    nd = len(shape)
    return pl.BlockSpec(shape, lambda *_: (0,) * nd, pipeline_mode=pl.Buffered(1))


def _inproj_kernel(x_ref, gain_ref, w_ref, convw_ref, wgu_ref, gbias_ref,
                   qn_ref, kvn_ref, wq_ref, wkv_ref, cos_ref, sin_ref, fb_ref,
                   oa_ref, oc_ref, qb_ref, kb_ref, vtb_ref, qd_ref, kd_ref, vtd_ref,
                   xe_ref, carry_ref, *, tm, tiles_per_seq):
    i = pl.program_id(0)
    h = _bf(_rms(x_ref[...], gain_ref[...]))

    qkv_w = 3 * BRANCH_WIDTH
    cw = BRANCH_WIDTH

    @pl.when(i % tiles_per_seq == 0)
    def _():
        xe_ref[0:8, :] = jnp.zeros((8, qkv_w), F32)
        carry_ref[...] = jnp.zeros_like(carry_ref)

    def conv_cols(c0):
        y = jnp.zeros((tm, cw), F32)
        for j in range(CONV_K):
            y = y + convw_ref[j:j + 1, c0:c0 + cw] * xe_ref[pl.ds(8 - (CONV_K - 1) + j, tm), c0:c0 + cw]
        oa_ref[:, c0:c0 + cw] = _silu(y)
        xe_ref[0:8, c0:c0 + cw] = xe_ref[tm:tm + 8, c0:c0 + cw]

    pcs = _dot(h, w_ref[:, OFF_C:OFF_D])
    oc_ref[:, 0:C_MAIN] = pcs[:, 0:C_MAIN]
    small = pcs[:, C_MAIN:C_MAIN + LANE]
    oa_ref[:, A_MAIN:A_WIDTH] = small
    pd = _dot(h, w_ref[:, OFF_D:OFF_D + D_WIDTH])
    pb = _dot(h, w_ref[:, OFF_B:OFF_B + B_WIDTH])
    fox = _fox_stages(pd, small, fb_ref, carry_ref, qd_ref, kd_ref, vtd_ref)
    mla = _mla_stages(pb, qn_ref, kvn_ref, wq_ref, wkv_ref, cos_ref, sin_ref, qb_ref, kb_ref, vtb_ref)

    xe_ref[8:8 + tm, 0:2 * cw] = _dot(h, w_ref[:, 0:2 * cw])
    for stage in fox:
        stage()
    pa = _dot(h, w_ref[:, 2 * cw:4 * cw])
    xe_ref[8:8 + tm, 2 * cw:3 * cw] = pa[:, 0:cw]
    oa_ref[:, 3 * cw:4 * cw] = pa[:, cw:2 * cw]
    conv_cols(0)
    conv_cols(cw)
    conv_cols(2 * cw)
    for stage in mla:
        stage()
    gate = _dot(_bf(small), wgu_ref[...]) + gbias_ref[...]
    oc_ref[:, C_MAIN:C_WIDTH] = _log_sigmoid(gate) * (1.0 / GLA_TAU)


def _inproj(x2, p, seq, tm):
    t = x2.shape[0]
    tps = seq // tm
    nslot = HEADS * SLOT
    row = lambda w: pl.BlockSpec((tm, w), lambda i: (i, 0))
    pos = pl.BlockSpec((tm, SLOT), lambda i: (i % tps, 0))
    vt_spec = pl.BlockSpec((1, HEADS * VT_ROWS, tm), lambda i: (i, 0, 0))
    slots = jax.ShapeDtypeStruct((t, nslot), BF16)
    vts = jax.ShapeDtypeStruct((t // tm, HEADS * VT_ROWS, tm), BF16)
    return pl.pallas_call(
        functools.partial(_inproj_kernel, tm=tm, tiles_per_seq=tps),
        grid=(t // tm,),
        in_specs=[row(D_MODEL), _const_spec((1, D_MODEL)), _const_spec((D_MODEL, IN_TOTAL)),
                  _const_spec((CONV_K, 3 * BRANCH_WIDTH)), _const_spec((LANE, LANE)),
                  _const_spec((1, LANE)),
                  _const_spec((1, MLA_Q_LORA)), _const_spec((1, MLA_KV_LORA)),
                  _const_spec((MLA_Q_LORA, 2 * nslot)), _const_spec((MLA_KV_LORA, nslot + BRANCH_WIDTH)),
                  pos, pos, _const_spec((1, LANE))],
        out_specs=[row(A_WIDTH), row(C_WIDTH), row(nslot), row(nslot), vt_spec,
                   row(nslot), row(nslot), vt_spec],
        out_shape=[jax.ShapeDtypeStruct((t, A_WIDTH), F32), jax.ShapeDtypeStruct((t, C_WIDTH), F32),
                   slots, slots, vts, slots, slots, vts],
        scratch_shapes=[pltpu.VMEM((tm + 8, 3 * BRANCH_WIDTH), F32), pltpu.VMEM((1, LANE), F32)],
        compiler_params=_params("arbitrary"),
        name="inproj",
    )(x2, p["norm_mix_pre"], p["w_cat"], p["dn_conv"], p["wgu"], p["gbias"],
      p["mla_q_norm"], p["mla_kv_norm"], p["wq"], p["wkv"], p["cos"], p["sin"], p["fbias"])


def _head_ones():
    r = lax.broadcasted_iota(jnp.int32, (BRANCH_WIDTH, BRANCH_WIDTH), 0)
    c = lax.broadcasted_iota(jnp.int32, (BRANCH_WIDTH, BRANCH_WIDTH), 1)
    shift = HEAD_DIM.bit_length() - 1
    return _bf(jnp.where((r >> shift) == (c >> shift), 1.0, 0.0))


def _head_sums(x, ones_bd):
    x1, x2 = _split(x, 2)
    return _dot(x1, ones_bd) + _dot(x2, ones_bd)


def _head_rms(x, ones_bd, gain):
    return x * lax.rsqrt(_head_sums(x * x, ones_bd) * (1.0 / HEAD_DIM) + EPS) * gain


def _deltanet_kernel(a_ref, c_ref, alog_ref, dtb_ref, norm_ref, gla_norm_ref, o_ref, oc_ref,
                     state_ref, gla_state_ref, *, tb, blocks_per_seq):
    i = pl.program_id(0)

    @pl.when(i % blocks_per_seq == 0)
    def _():
        state_ref[...] = jnp.zeros_like(state_ref)
        gla_state_ref[...] = jnp.zeros_like(gla_state_ref)

    gla_stages, gla_chunk_step, gla_finish = _gla_parts(c_ref, gla_norm_ref, oc_ref, gla_state_ref, tb)

    bw = BRANCH_WIDTH
    nchunk = tb // CHUNK
    hshift = HEAD_DIM.bit_length() - 1
    t_idx = lax.broadcasted_iota(jnp.int32, (CHUNK, bw), 0)
    lane = lax.broadcasted_iota(jnp.int32, (CHUNK, bw), 1)
    s_idx = lane & (HEAD_DIM - 1)
    lower = s_idx <= t_idx
    strict = s_idx < t_idx
    same_block = lambda log2s: (t_idx >> log2s) == (s_idx >> log2s)
    eye = jnp.where(s_idx == t_idx, 1.0, 0.0).astype(F32)
    strict_f = jnp.where(strict, 1.0, 0.0).astype(F32)
    head_sel = [_bf(jnp.where((lane >> hshift) == h, 1.0, 0.0)) for h in range(HEADS)]
    r64 = lax.broadcasted_iota(jnp.int32, (CHUNK, CHUNK), 0)
    c64 = lax.broadcasted_iota(jnp.int32, (CHUNK, CHUNK), 1)
    lmat = _bf(jnp.where(c64 <= r64, 1.0, 0.0))
    r256 = lax.broadcasted_iota(jnp.int32, (bw, bw), 0)
    c256 = lax.broadcasted_iota(jnp.int32, (bw, bw), 1)
    same_head = (r256 >> hshift) == (c256 >> hshift)
    ones_bd = _bf(jnp.where(same_head, 1.0, 0.0))

    def bd(y):
        return jnp.concatenate([y * m for m in head_sel], axis=0)

    def mm(a, b):
        return _dot(a, bd(b))

    def mm3(a, b):
        ah, al = _split(a, 2)
        bh, bl = _split(b, 2)
        bdh = bd(bh)
        return _dot(ah, bdh) + (_dot(ah, bd(bl)) + _dot(al, bdh))

    small = a_ref[:, A_MAIN:A_WIDTH]
    g_small = -jnp.exp(alog_ref[...]) * _softplus(small + dtb_ref[...])
    b_small = jax.nn.sigmoid(small)
    head_of_lane = lax.broadcasted_iota(jnp.int32, (tb, bw), 1) >> hshift

    def per_head(x, lane0):
        out = jnp.broadcast_to(x[:, lane0 + HEADS - 1:lane0 + HEADS], (tb, bw))
        for h in range(HEADS - 2, -1, -1):
            out = jnp.where(head_of_lane == h, jnp.broadcast_to(x[:, lane0 + h:lane0 + h + 1], (tb, bw)), out)
        return out

    g_all = per_head(g_small, SM_A)
    beta_all = per_head(b_small, SM_B)
    q_all = a_ref[:, 0:bw]
    k_all = a_ref[:, bw:2 * bw]
    q_all = q_all * lax.rsqrt(_head_sums(q_all * q_all, ones_bd) + EPS) * (HEAD_DIM ** -0.5)
    k_all = k_all * lax.rsqrt(_head_sums(k_all * k_all, ones_bd) + EPS)
    vb_all = a_ref[:, 2 * bw:3 * bw] * beta_all

    chunks = range(nchunk)
    rows = lambda x, c: x[c * CHUNK:(c + 1) * CHUNK, :]
    qs = [rows(q_all, c) for c in chunks]
    ks = [rows(k_all, c) for c in chunks]
    gs = [rows(g_all, c) for c in chunks]
    betas = [rows(beta_all, c) for c in chunks]

    cums = []
    for g in gs:
        g1, g2 = _split(jnp.concatenate([g * strict_f, g], axis=1), 2)
        cums.append(_dot(lmat, g1) + _dot(lmat, g2))
    diffs = [cm[:, 0:bw] for cm in cums]
    gcs = [cm[:, bw:2 * bw] for cm in cums]
    decays = [jnp.where(lower, jnp.exp(jnp.where(lower, d, 0.0)), 0.0) for d in diffs]
    kbs = [_bf(k) for k in ks]
    kqs = [_dot_nt(jnp.concatenate([kb, _bf(q)], axis=0), bd(kb)) for kb, q in zip(kbs, qs)]
    amats = [jnp.where(strict, b * kq[0:CHUNK] * d, 0.0) for b, kq, d in zip(betas, kqs, decays)]

    xs = [eye - jnp.where(same_block(1), a, 0.0) for a in amats]
    for log2s in range(1, CHUNK.bit_length() - 1):
        off = same_block(log2s + 1) & ~same_block(log2s)
        xbs = [_bf(x) for x in xs]
        xns = [_bf(mm(xb, _bf(jnp.where(off, a, 0.0)))) for xb, a in zip(xbs, amats)]
        if log2s <= len(gla_stages):
            gla_stages[log2s - 1]()
        xs = [x - mm(xn, xb) for x, xn, xb in zip(xs, xns, xbs)]
    resid = [eye - mm3(eye + a, x) for a, x in zip(amats, xs)]
    xs = [x + mm(_bf(x), _bf(r)) for x, r in zip(xs, resid)]

    egcs = [jnp.exp(gc) for gc in gcs]
    us = [mm3(x, rows(vb_all, c)) for x, c in zip(xs, chunks)]
    wbs = [_bf(mm(_bf(x), _bf(k * b * e))) for x, k, b, e in zip(xs, ks, betas, egcs)]
    qks = [_bf(kq[CHUNK:2 * CHUNK] * d) for kq, d in zip(kqs, decays)]
    q_decays = [_bf(q * e) for q, e in zip(qs, egcs)]
    lasts = [gc[CHUNK - 1:CHUNK, :] for gc in gcs]
    k_tails = [_bf(k * jnp.exp(l - gc)) for k, l, gc in zip(ks, lasts, gcs)]
    cds = [jnp.exp(l) for l in lasts]

    s = state_ref[...]
    o_rows = []
    for c in chunks:
        sb = _bf(s)
        vb = _bf(us[c] - _dot(wbs[c], sb))
        gla_chunk_step(c)
        o_rows.append(_dot(q_decays[c], sb) + mm(qks[c], vb))
        s = s * cds[c] + jnp.where(same_head, _dot_tn(k_tails[c], vb), 0.0)
    state_ref[...] = s
    gla_finish()
    o_all = jnp.concatenate(o_rows, axis=0)
    o_ref[...] = _bf(_head_rms(o_all, ones_bd, norm_ref[...]) * _silu(a_ref[:, 3 * bw:4 * bw]))


def _recurrent_mixers(ga, gc, alog, dtb, dn_norm, gla_norm, seq, tb=512):
    t = ga.shape[0]
    out = jax.ShapeDtypeStruct((t, BRANCH_WIDTH), BF16)
    out_spec = pl.BlockSpec((tb, BRANCH_WIDTH), lambda i: (i, 0))
    return pl.pallas_call(
        functools.partial(_deltanet_kernel, tb=tb, blocks_per_seq=seq // tb),
        grid=(t // tb,),
        in_specs=[pl.BlockSpec((tb, A_WIDTH), lambda i: (i, 0)),
                  pl.BlockSpec((tb, C_WIDTH), lambda i: (i, 0)),
                  _const_spec((1, LANE)), _const_spec((1, LANE)),
                  _const_spec((1, BRANCH_WIDTH)), _const_spec((1, BRANCH_WIDTH))],
        out_specs=[out_spec, out_spec],
        out_shape=[out, out],
        scratch_shapes=[pltpu.VMEM((BRANCH_WIDTH, BRANCH_WIDTH), F32),
                        pltpu.VMEM((BRANCH_WIDTH, HEADS * GLA_DK), F32)],
        compiler_params=_params("arbitrary"),
        name="deltanet_gla",
    )(ga, gc, alog, dtb, dn_norm, gla_norm)


def _gla_parts(c_ref, norm_ref, o_ref, state_ref, tb):
    qw = HEADS * GLA_DK
    bw = BRANCH_WIDTH
    nchunk = tb // CHUNK
    la_off = 2 * qw + 2 * bw
    kshift = GLA_DK.bit_length() - 1
    vshift = HEAD_DIM.bit_length() - 1
    r64 = lax.broadcasted_iota(jnp.int32, (CHUNK, CHUNK), 0)
    c64 = lax.broadcasted_iota(jnp.int32, (CHUNK, CHUNK), 1)
    lmat = _bf(jnp.where(c64 <= r64, 1.0, 0.0))
    t_idx = lax.broadcasted_iota(jnp.int32, (CHUNK, bw), 0)
    lane_v = lax.broadcasted_iota(jnp.int32, (CHUNK, bw), 1)
    lane_k = lax.broadcasted_iota(jnp.int32, (CHUNK, qw), 1)
    lower = (lane_v & (HEAD_DIM - 1)) <= t_idx
    sel_v = [_bf(jnp.where((lane_v >> vshift) == h, 1.0, 0.0)) for h in range(HEADS)]
    sel_k = [_bf(jnp.where((lane_k >> kshift) == h, 1.0, 0.0)) for h in range(HEADS)]
    same_head = (lax.broadcasted_iota(jnp.int32, (bw, qw), 0) >> vshift) == \
                (lax.broadcasted_iota(jnp.int32, (bw, qw), 1) >> kshift)

    def bd(y, sel):
        return jnp.concatenate([y * m for m in sel], axis=0)

    chunks = range(nchunk)
    rows = lambda col0, width, c: c_ref[c * CHUNK:(c + 1) * CHUNK, col0:col0 + width]

    v = {}

    def cumulative_gate():
        v["bcum"] = []
        for c in chunks:
            a1, a2 = _split(rows(la_off, qw, c), 2)
            v["bcum"].append(_dot(lmat, a1) + _dot(lmat, a2))

    def scaled_operands():
        v["q_in"] = [_bf(rows(0, qw, c) * (GLA_DK ** -0.5) * jnp.exp(b)) for c, b in zip(chunks, v["bcum"])]
        v["k_in"] = [_bf(rows(qw, qw, c) * jnp.exp(-b)) for c, b in zip(chunks, v["bcum"])]
        v["vb"] = [_bf(rows(2 * qw, bw, c)) for c in chunks]

    def intra_scores():
        v["attn"] = [_bf(jnp.where(lower, _dot_nt(qi, bd(ki, sel_k)), 0.0))
                     for qi, ki in zip(v["q_in"], v["k_in"])]

    def intra_outputs():
        v["intra"] = [_dot(a, bd(vb, sel_v)) for a, vb in zip(v["attn"], v["vb"])]
        lasts = [b[CHUNK - 1:CHUNK, :] for b in v["bcum"]]
        v["k_tail"] = [_bf(rows(qw, qw, c) * jnp.exp(l - b)) for c, l, b in zip(chunks, lasts, v["bcum"])]
        v["cd"] = [jnp.exp(l) for l in lasts]
        v["state"] = state_ref[...]
        v["o_rows"] = []

    def chunk_step(c):
        st = v["state"]
        v["o_rows"].append(v["intra"][c] + _dot_nt(v["q_in"][c], _bf(st)))
        v["state"] = st * v["cd"][c] + jnp.where(same_head, _dot_tn(v["vb"][c], v["k_tail"][c]), 0.0)

    def finish():
        state_ref[...] = v["state"]
        o_all = jnp.concatenate(v["o_rows"], axis=0)
        gate = c_ref[:, 2 * qw + bw:2 * qw + 2 * bw]
        o_ref[...] = _bf(_head_rms(o_all, _head_ones(), norm_ref[...]) * _silu(gate))

    return [cumulative_gate, scaled_operands, intra_scores, intra_outputs], chunk_step, finish


VT_ROWS = HEAD_DIM + 16


def _store_vt(vt_ref, v):
    vt = v.T
    tokens = v.shape[0]
    pad_rows = VT_ROWS - HEAD_DIM
    ones_row = _bf(jnp.where(lax.broadcasted_iota(jnp.int32, (pad_rows, tokens), 0) == 0, 1.0, 0.0))
    for h in range(HEADS):
        vt_ref[0, h * VT_ROWS:h * VT_ROWS + HEAD_DIM, :] = _bf(vt[h * HEAD_DIM:(h + 1) * HEAD_DIM, :])
        vt_ref[0, h * VT_ROWS + HEAD_DIM:(h + 1) * VT_ROWS, :] = ones_row


def _mla_stages(pb, qn_ref, kvn_ref, wq_ref, wkv_ref, cos_ref, sin_ref, q_ref, k_ref, vt_ref):
    nslot = HEADS * SLOT
    st = {}

    def prologue():
        cq = pb[:, 0:MLA_Q_LORA]
        ckv = pb[:, MLA_Q_LORA:MLA_Q_LORA + MLA_KV_LORA]
        off = MLA_Q_LORA + MLA_KV_LORA
        st["cos"] = cos_ref[...]
        st["sin"] = sin_ref[...]
        st["qa"] = _dot(_bf(_rms(cq, qn_ref[...])), wq_ref[...])
        st["kv"] = _dot(_bf(_rms(ckv, kvn_ref[...])), wkv_ref[...])
        kslot = pb[:, off:off + SLOT]
        st["k_pe"] = kslot * st["cos"] + pltpu.roll(kslot, SLOT - MLA_ROPE, axis=1) * st["sin"]

    def head(h):
        def run():
            sl = slice(h * SLOT, (h + 1) * SLOT)
            qa = st["qa"]
            q_rope = qa[:, sl] * st["cos"] + qa[:, nslot + h * SLOT:nslot + (h + 1) * SLOT] * st["sin"]
            q_ref[:, sl] = _bf(q_rope * ((MLA_NOPE + MLA_ROPE) ** -0.5 * LOG2E))
            k_ref[:, sl] = _bf(st["kv"][:, sl] + st["k_pe"])
        return run

    def store_v():
        _store_vt(vt_ref, st["kv"][:, nslot:nslot + BRANCH_WIDTH])

    return [prologue] + [head(h) for h in range(HEADS)] + [store_v]


def _fox_stages(pd, small, fb_ref, carry_ref, q_ref, k_ref, vt_ref):
    tb = pd.shape[0]
    bw = BRANCH_WIDTH
    scale = HEAD_DIM ** -0.5 * LOG2E
    st = {}

    def prologue():
        log_f = _log_sigmoid(small + fb_ref[...])
        lmat = _bf(jnp.where(_tri(tb), 1.0, 0.0))
        fcum = _dot_ones(lmat, log_f) + carry_ref[...]
        carry_ref[...] = fcum[tb - 1:tb, :]
        st["f"] = [f.astype(F32) for f in _split(fcum * LOG2E, 3)]

    def head(h):
        def run():
            lane = lax.broadcasted_iota(jnp.int32, (tb, LANE), 1)
            pair, odd = divmod(h, 2)
            qp = pd[:, pair * LANE:(pair + 1) * LANE] * scale
            kp = pd[:, bw + pair * LANE:bw + (pair + 1) * LANE]
            head_lanes = (lane >= HEAD_DIM) if odd else (lane < HEAD_DIM)
            a0 = 0 if odd else HEAD_DIM
            fb = [jnp.broadcast_to(f[:, SM_F + h:SM_F + h + 1], (tb, LANE)) for f in st["f"]]
            lane_row = lax.broadcasted_iota(jnp.int32, (1, LANE), 1)
            at = lambda lo, n=1: jnp.where((lane_row >= lo) & (lane_row < lo + n), 1.0, 0.0).astype(F32)
            qx = fb[0] * at(a0) + fb[1] * at(a0 + 1) + fb[2] * at(a0 + 2) + at(a0 + 3, 3)
            kx = at(a0, 3) - (fb[0] * at(a0 + 3) + fb[1] * at(a0 + 4) + fb[2] * at(a0 + 5))
            q_ref[:, h * SLOT:(h + 1) * SLOT] = _bf(jnp.where(head_lanes, qp, qx))
            k_ref[:, h * SLOT:(h + 1) * SLOT] = _bf(jnp.where(head_lanes, kp, kx))
        return run

    def store_v():
        _store_vt(vt_ref, pd[:, 2 * bw:3 * bw])

    return [prologue] + [head(h) for h in range(HEADS)] + [store_v]


def _flash_kernel(q_ref, k_ref, vt_ref, o_ref, m_ref, acc_ref, s_ref, *, chunk_mask, tq, tk):
    assert tq == tk
    qi = pl.program_id(1)
    m_ref[...] = jnp.full_like(m_ref, -jnp.inf)
    acc_ref[...] = jnp.zeros_like(acc_ref)

    def scores(ki, slot, hh):
        s_ref[slot, hh] = _dot_nt(k_ref[ki, :, hh * SLOT:(hh + 1) * SLOT],
                                  q_ref[:, hh * SLOT:(hh + 1) * SLOT])

    def softmax_pv(ki, slot, hh, masked):
        s = s_ref[slot, hh]
        if masked:
            kpos = lax.broadcasted_iota(jnp.int32, s.shape, 0)
            qpos = lax.broadcasted_iota(jnp.int32, s.shape, 1)
            if chunk_mask:
                shift = CHUNK.bit_length() - 1
                ok = (kpos >> shift) <= (qpos >> shift)
            else:
                ok = kpos <= qpos
            s = jnp.where(ok, s, -jnp.inf)
        m_prev = m_ref[hh]
        m_new = jnp.maximum(m_prev, jnp.max(s, axis=0, keepdims=True))
        p = _bf(jnp.exp2(s - m_new))
        pv = _dot(vt_ref[ki, hh * VT_ROWS:(hh + 1) * VT_ROWS, :], p)
        acc_ref[hh] = jnp.exp2(m_prev - m_new) * acc_ref[hh] + pv
        m_ref[hh] = m_new

    def step(ki, slot, masked=False, prefetch=True):
        for hh in range(HEADS):
            if prefetch:
                scores(ki + 1, 1 - slot, hh)
            softmax_pv(ki, slot, hh, masked)

    for hh in range(HEADS):
        scores(0, 0, hh)

    def pair(j, carry):
        step(2 * j, 0)
        step(2 * j + 1, 1)
        return carry

    lax.fori_loop(0, qi // 2, pair, 0)

    @pl.when(qi % 2 == 0)
    def _():
        step(qi, 0, masked=True, prefetch=False)

    @pl.when(qi % 2 == 1)
    def _():
        step(qi - 1, 0)
        step(qi, 1, masked=True, prefetch=False)

    ot = jnp.concatenate([acc_ref[hh][0:HEAD_DIM] / acc_ref[hh][HEAD_DIM:HEAD_DIM + 1]
                          for hh in range(HEADS)], axis=0)
    o_ref[...] = _bf(ot.T)


def _flash(q, k, vt3, batch, seq, chunk_mask, tq):
    tk = tq
    nq = seq // tq
    nk = seq // tk
    t = q.shape[0]
    k3 = k.reshape(t // tk, tk, HEADS * SLOT)
    return pl.pallas_call(
        functools.partial(_flash_kernel, chunk_mask=chunk_mask, tq=tq, tk=tk),
        grid=(batch, nq),
        in_specs=[pl.BlockSpec((tq, HEADS * SLOT), lambda b, qi: (b * nq + qi, 0)),
                  pl.BlockSpec((nk, tk, HEADS * SLOT), lambda b, qi: (b, 0, 0)),
                  pl.BlockSpec((nk, HEADS * VT_ROWS, tk), lambda b, qi: (b, 0, 0))],
        out_specs=pl.BlockSpec((tq, BRANCH_WIDTH), lambda b, qi: (b * nq + qi, 0)),
        out_shape=jax.ShapeDtypeStruct((t, BRANCH_WIDTH), BF16),
        scratch_shapes=[pltpu.VMEM((HEADS, 1, tq), F32), pltpu.VMEM((HEADS, VT_ROWS, tq), F32),
                        pltpu.VMEM((2, HEADS, tk, tq), F32)],
        compiler_params=_params("parallel", "arbitrary"),
        name="flash_chunk" if chunk_mask else "flash_causal",
    )(q, k3, vt3)


def _merge_kernel(x_ref, oa_ref, ob_ref, oc_ref, od_ref, gpre_ref, wg_ref, bg_ref, wup_ref,
                  wout_ref, gpost_ref, o_ref):
    x = x_ref[...]
    h = _bf(_rms(x, gpre_ref[...]))
    merged = jnp.zeros(x.shape, F32)
    for i, br in enumerate((oa_ref, ob_ref, oc_ref, od_ref)):
        gate = jax.nn.sigmoid(_dot(h, wg_ref[i]) + bg_ref[i:i + 1, :])
        merged = merged + gate * _dot(br[...], wup_ref[i])
    y = _dot(_bf(merged), wout_ref[...])
    o_ref[...] = x + _rms(y, gpost_ref[...])


def _merge(x2, oa, ob, oc, od, gpre, wg, bg, wup, wout, gpost, tm=512):
    t = x2.shape[0]
    row = lambda w: pl.BlockSpec((tm, w), lambda i: (i, 0))
    return pl.pallas_call(
        _merge_kernel,
        grid=(t // tm,),
        in_specs=[row(D_MODEL)] + [row(BRANCH_WIDTH)] * 4 +
                 [_const_spec((1, D_MODEL)), _const_spec((4, D_MODEL, D_MODEL)),
                  _const_spec((4, D_MODEL)), _const_spec((4, BRANCH_WIDTH, D_MODEL)),
                  _const_spec((D_MODEL, D_MODEL)), _const_spec((1, D_MODEL))],
        out_specs=row(D_MODEL),
        out_shape=jax.ShapeDtypeStruct((t, D_MODEL), F32),
        compiler_params=_params("parallel"),
        name="merge",
    )(x2, oa, ob, oc, od, gpre, wg, bg, wup, wout, gpost)


def _mlp_kernel(x_ref, gpre_ref, w1_ref, w2_ref, gpost_ref, o_ref, *, ff_chunk):
    x = x_ref[...]
    h = _bf(_rms(x, gpre_ref[...]))
    y = jnp.zeros(x.shape, F32)
    for c in range(D_FF // ff_chunk):
        u = jnp.maximum(_dot(h, w1_ref[:, c * ff_chunk:(c + 1) * ff_chunk]), 0.0)
        y = y + _dot(_bf(u * u), w2_ref[c * ff_chunk:(c + 1) * ff_chunk, :])
    o_ref[...] = x + _rms(y, gpost_ref[...])


def _mlp(x2, gpre, w1, w2, gpost, tm=512, ff_chunk=1024):
    t = x2.shape[0]
    row = pl.BlockSpec((tm, D_MODEL), lambda i: (i, 0))
    return pl.pallas_call(
        functools.partial(_mlp_kernel, ff_chunk=ff_chunk),
        grid=(t // tm,),
        in_specs=[row, _const_spec((1, D_MODEL)), _const_spec((D_MODEL, D_FF)),
                  _const_spec((D_FF, D_MODEL)), _const_spec((1, D_MODEL))],
        out_specs=row,
        out_shape=jax.ShapeDtypeStruct((t, D_MODEL), F32),
        compiler_params=_params("parallel"),
        name="mlp",
    )(x2, gpre, w1, w2, gpost)


_IN_SIZES = (
    HEADS * 3 * HEAD_DIM, HEADS * HEAD_DIM, HEADS, HEADS,
    MLA_Q_LORA, MLA_KV_LORA, MLA_ROPE,
    HEADS * GLA_DK, HEADS * GLA_DK, HEADS * HEAD_DIM, HEADS * HEAD_DIM, GLA_RANK,
    HEADS * HEAD_DIM, HEADS * HEAD_DIM, HEADS * HEAD_DIM, HEADS,
)


def _rot_cols(w):
    half = w.shape[-1] // 2
    return jnp.concatenate([-w[..., half:], w[..., :half]], axis=-1)


def _pad_cols(w, width):
    return jnp.pad(w, ((0, 0), (0, width - w.shape[-1])))


def _layout_in_weight(w_in):
    splits = np.cumsum(_IN_SIZES)[:-1].tolist()
    (dn_qkv, dn_z, dn_b, dn_a, cq, ckv, kpe, gq, gk, gv, gg, glr, fq, fk, fv, ff) = jnp.split(w_in, splits, axis=1)
    zeros = lambda n: jnp.zeros((w_in.shape[0], n), w_in.dtype)
    assert (SM_A, SM_B, SM_LR, SM_F) == (0, HEADS, 2 * HEADS, 2 * HEADS + GLA_RANK)
    small = _pad_cols(jnp.concatenate([dn_a, dn_b, glr, ff], axis=1), LANE)
    cols = [dn_qkv, dn_z,
            cq, ckv, zeros(MLA_NOPE), kpe, _rot_cols(kpe),
            gq, gk, gv, gg, small,
            fq, fk, fv]
    w = jnp.concatenate([_bf(c) for c in cols], axis=1)
    assert w.shape[1] == IN_TOTAL
    return w


def _layout_q_up(w):
    zeros = lambda n: jnp.zeros((w.shape[0], n), w.dtype)
    per = MLA_NOPE + MLA_ROPE
    plain, rot = [], []
    for h in range(HEADS):
        nope = w[:, h * per:h * per + MLA_NOPE]
        pe = w[:, h * per + MLA_NOPE:(h + 1) * per]
        plain += [nope, pe, zeros(SLOT - per)]
        rot += [zeros(MLA_NOPE), _rot_cols(pe), zeros(SLOT - per)]
    return _bf(jnp.concatenate(plain + rot, axis=1))


def _layout_kv_up(w):
    zeros = lambda n: jnp.zeros((w.shape[0], n), w.dtype)
    per = MLA_NOPE + HEAD_DIM
    ks, vs = [], []
    for h in range(HEADS):
        ks += [w[:, h * per:h * per + MLA_NOPE], zeros(SLOT - MLA_NOPE)]
        vs += [w[:, h * per + MLA_NOPE:(h + 1) * per]]
    return _bf(jnp.concatenate(ks + vs, axis=1))


def _rope_tables(seq):
    inv_freq = 1.0 / (ROPE_THETA ** (jnp.arange(0, MLA_ROPE, 2, dtype=F32) / MLA_ROPE))
    ang = jnp.arange(seq, dtype=F32)[:, None] * inv_freq[None, :]
    ang = jnp.concatenate([ang, ang], axis=-1)
    pad = jnp.zeros((seq, SLOT - MLA_NOPE - MLA_ROPE), F32)
    cos_t = jnp.concatenate([jnp.ones((seq, MLA_NOPE), F32), jnp.cos(ang), pad], axis=1)
    sin_t = jnp.concatenate([jnp.zeros((seq, MLA_NOPE), F32), jnp.sin(ang), pad], axis=1)
    return cos_t, sin_t


TOKEN_TILE = 512


def _layer(x2, batch, seq, p):
    ga, gc, q_b, k_b, vt_b, q_d, k_d, vt_d = _inproj(x2, p, seq, TOKEN_TILE)
    o_a, o_c = _recurrent_mixers(ga, gc, p["alog"], p["dtb"], p["dn_norm"], p["gla_norm"], seq)
    o_b = _flash(q_b, k_b, vt_b, batch, seq, True, TOKEN_TILE)
    o_d = _flash(q_d, k_d, vt_d, batch, seq, False, TOKEN_TILE)
    x2 = _merge(x2, o_a, o_b, o_c, o_d, p["norm_mix_pre"], p["w_gate"], p["b_gate"],
                p["w_branch_up"], p["w_out"], p["norm_mix_post"])
    return _mlp(x2, p["norm_mlp_pre"], p["w_mlp_in"], p["w_mlp_out"], p["norm_mlp_post"])


def kernel(x, norm_mix_pre, norm_mix_post, norm_mlp_pre, norm_mlp_post, w_in, dn_conv, dn_a_log, dn_dt_bias, dn_norm, mla_q_norm, mla_w_q_up, mla_kv_norm, mla_w_kv_up, gla_w_gate_up, gla_gate_bias, gla_norm, fox_f_bias, w_branch_up, w_gate, b_gate, w_out, w_mlp_in, w_mlp_out):
    batch, seq, d = x.shape
    depth = w_in.shape[0]
    cos_t, sin_t = _rope_tables(seq)
    x2 = x.reshape(batch * seq, d)
    row = lambda a: a.reshape(1, -1)
    lanes_at = lambda a, lane0: row(jnp.pad(a, (lane0, LANE - lane0 - a.shape[0])))
    for l in range(depth):
        p = {
            "norm_mix_pre": row(norm_mix_pre[l]), "norm_mix_post": row(norm_mix_post[l]),
            "norm_mlp_pre": row(norm_mlp_pre[l]), "norm_mlp_post": row(norm_mlp_post[l]),
            "w_cat": _layout_in_weight(w_in[l]),
            "dn_conv": dn_conv[l],
            "wgu": _bf(jnp.pad(gla_w_gate_up[l], ((SM_LR, LANE - SM_LR - GLA_RANK), (0, 0)))),
            "gbias": row(gla_gate_bias[l]),
            "alog": lanes_at(dn_a_log[l], SM_A), "dtb": lanes_at(dn_dt_bias[l], SM_A),
            "dn_norm": row(jnp.tile(dn_norm[l], HEADS)),
            "mla_q_norm": row(mla_q_norm[l]), "mla_kv_norm": row(mla_kv_norm[l]),
            "wq": _layout_q_up(mla_w_q_up[l]), "wkv": _layout_kv_up(mla_w_kv_up[l]),
            "cos": cos_t, "sin": sin_t,
            "gla_norm": row(jnp.tile(gla_norm[l], HEADS)),
            "fbias": lanes_at(fox_f_bias[l], SM_F),
            "w_gate": _bf(w_gate[l]), "b_gate": b_gate[l], "w_branch_up": _bf(w_branch_up[l]),
            "w_out": _bf(w_out[l]), "w_mlp_in": _bf(w_mlp_in[l]), "w_mlp_out": _bf(w_mlp_out[l]),
        }
        x2 = _layer(x2, batch, seq, p)
    return x2.reshape(batch, seq, d)
```
